```python
import jax, jax.numpy as jnp
from jax import lax
import numpy as np

D_MODEL = 2048
BATCH = 8
SEQ = 2048
DEPTH = 1
DEC_BATCH = 128
DEC_SEQ = 8
PAST_LEN = 8192
PAGE_SIZE = 128

N_META = 16
D_MIX = D_MODEL
D_ATTN = D_MIX // 2
D_CONV = D_MIX - D_ATTN
HEAD_DIM = 64
N_HEADS = D_ATTN // HEAD_DIM
N_KV_HEADS = max(1, N_HEADS // 8)
GQA_GROUP = N_HEADS // N_KV_HEADS
KV_WIDTH = N_KV_HEADS * HEAD_DIM
WINDOW = 128
BLOCK = 128
ROPE_DIM = HEAD_DIM // 4
ROPE_THETA = 500000.0
CONV_WIDTH = 31
D_PROJ = 2 * D_ATTN + 2 * KV_WIDTH + 3 * D_CONV
EPS = 1e-5

kernel_name = 'hymba_conformer_swa_sink_step'


def rms_norm(x, g):
    xf = x.astype(jnp.float32)
    y = xf * lax.rsqrt(jnp.mean(xf * xf, axis=-1, keepdims=True) + EPS)
    return (y * g.astype(jnp.float32)).astype(x.dtype)


def layer_norm(x, g, b):
    xf = x.astype(jnp.float32)
    mu = jnp.mean(xf, axis=-1, keepdims=True)
    xc = xf - mu
    y = xc * lax.rsqrt(jnp.mean(xc * xc, axis=-1, keepdims=True) + EPS)
    return (y * g.astype(jnp.float32) + b.astype(jnp.float32)).astype(x.dtype)


def rope(x, pos):
    half = ROPE_DIM // 2
    inv = ROPE_THETA ** (-jnp.arange(0, ROPE_DIM, 2, dtype=jnp.float32) / ROPE_DIM)
    ang = pos.astype(jnp.float32)[:, None] * inv[None, :]
    bshape = (pos.shape[0],) + (1,) * (x.ndim - 3) + (half,)
    cos = jnp.cos(ang).reshape(bshape)
    sin = jnp.sin(ang).reshape(bshape)
    xr = x[..., :ROPE_DIM].astype(jnp.float32)
    x1, x2 = xr[..., :half], xr[..., half:]
    rot = jnp.concatenate([x1 * cos - x2 * sin, x2 * cos + x1 * sin], axis=-1).astype(x.dtype)
    return jnp.concatenate([rot, x[..., ROPE_DIM:]], axis=-1)


def project(h, g_norm, w_in):
    u = rms_norm(h, g_norm)
    p = jnp.einsum('btd,dp->btp', u, w_in)
    cuts = [D_ATTN, D_ATTN + KV_WIDTH, D_ATTN + 2 * KV_WIDTH, 2 * D_ATTN + 2 * KV_WIDTH,
            2 * D_ATTN + 2 * KV_WIDTH + D_CONV, 2 * D_ATTN + 2 * KV_WIDTH + 2 * D_CONV]
    q, k, v, g_attn, glu_a, glu_b, g_conv = jnp.split(p, cuts, axis=-1)
    B, T = h.shape[0], h.shape[1]
    q = q.reshape(B, T, N_KV_HEADS, GQA_GROUP, HEAD_DIM)
    k = k.reshape(B, T, N_KV_HEADS, HEAD_DIM)
    v = v.reshape(B, T, N_KV_HEADS, HEAD_DIM)
    return q, k, v, g_attn, glu_a, glu_b, g_conv


def sink_attention(q, k, v, mask, sinks):
    s = jnp.einsum('...qkgd,...skd->...kgqs', q, k).astype(jnp.float32) * (HEAD_DIM ** -0.5)
    m = mask[..., None, None, :, :]
    sink = sinks.astype(jnp.float32).reshape(N_KV_HEADS, GQA_GROUP)[:, :, None, None]
    mx = jnp.maximum(jnp.max(jnp.where(m, s, -jnp.inf), axis=-1, keepdims=True), sink)
    p = jnp.where(m, jnp.exp(s - mx), 0.0)
    denom = jnp.sum(p, axis=-1, keepdims=True) + jnp.exp(sink - mx)
    p = (p / denom).astype(v.dtype)
    return jnp.einsum('...kgqs,...skd->...qkgd', p, v)


def prompt_window_attention(q, k, v, sinks):
    B, L = q.shape[0], q.shape[1]
    pad = (-L) % BLOCK
    Lp = L + pad
    nb = Lp // BLOCK

    def padf(t):
        return jnp.pad(t, ((0, 0), (pad, 0)) + ((0, 0),) * (t.ndim - 2))

    def band(t):
        prev = jnp.concatenate([jnp.zeros_like(t[:, :1]), t[:, :-1]], axis=1)
        return jnp.concatenate([prev, t], axis=2)

    qb = padf(q).reshape(B, nb, BLOCK, N_KV_HEADS, GQA_GROUP, HEAD_DIM)
    kb = padf(k).reshape(B, nb, BLOCK, N_KV_HEADS, HEAD_DIM)
    vb = padf(v).reshape(B, nb, BLOCK, N_KV_HEADS, HEAD_DIM)
    qpos = (jnp.arange(Lp) - pad).reshape(nb, BLOCK)
    kpos = jnp.concatenate([qpos - BLOCK, qpos], axis=1)
    diff = qpos[:, :, None] - kpos[:, None, :]
    mask = (kpos[:, None, :] >= 0) & (diff >= 0) & (diff <= WINDOW)
    o = sink_attention(qb, band(kb), band(vb), mask, sinks)
    return o.reshape(B, Lp, D_ATTN)[:, pad:]


def sample_window_attention(q, k, v, cache_k, cache_v, sinks):
    B, T = q.shape[0], q.shape[1]
    wc = cache_k.shape[1]
    kk = jnp.concatenate([cache_k, k], axis=1)
    vv = jnp.concatenate([cache_v, v], axis=1)
    qpos = PAST_LEN + jnp.arange(T)
    kpos = PAST_LEN - wc + jnp.arange(wc + T)
    diff = qpos[:, None] - kpos[None, :]
    mask = (diff >= 0) & (diff <= WINDOW)
    o = sink_attention(q, kk, vv, mask, sinks)
    return o.reshape(B, T, D_ATTN), kk[:, -WINDOW:], vv[:, -WINDOW:]


def conv_branch(glu_a, glu_b, prefix, w_dw, b_dw, ln_g, ln_b):
    u = glu_a * jax.nn.sigmoid(glu_b)
    xs = jnp.concatenate([prefix.astype(u.dtype), u], axis=1)
    y = lax.conv_general_dilated(xs, w_dw[:, None, :].astype(u.dtype), window_strides=(1,), padding='VALID',
                                 dimension_numbers=('NWC', 'WIO', 'NWC'), feature_group_count=D_CONV)
    y = y + b_dw
    y = jax.nn.silu(layer_norm(y, ln_g, ln_b))
    return y, xs[:, -(CONV_WIDTH - 1):]


def merge(attn_out, g_attn, conv_out, g_conv, w_out):
    z = jnp.concatenate([attn_out * jax.nn.silu(g_attn), conv_out * jax.nn.silu(g_conv)], axis=-1)
    return jnp.einsum('btm,md->btd', z, w_out)


def setup_inputs(seed: int = 0) -> dict:
    key = jax.random.key(seed)
    ks = jax.random.split(key, 16)
    f32 = jnp.float32
    wc = min(WINDOW, PAST_LEN)
    return {
        'x_prompt': jax.random.normal(ks[0], (BATCH, SEQ, D_MODEL), f32),
        'x_sample': jax.random.normal(ks[1], (DEC_BATCH, DEC_SEQ, D_MODEL), f32),
        'cache_k_win': jax.random.normal(ks[2], (DEPTH, DEC_BATCH, wc, N_KV_HEADS, HEAD_DIM), f32),
        'cache_v_win': jax.random.normal(ks[3], (DEPTH, DEC_BATCH, wc, N_KV_HEADS, HEAD_DIM), f32),
        'state_conv': 0.5 * jax.random.normal(ks[4], (DEPTH, DEC_BATCH, CONV_WIDTH - 1, D_CONV), f32),
        'meta_tokens': jax.random.normal(ks[5], (N_META, D_MODEL), f32),
        'norm_gain': 1.0 + 0.02 * jax.random.normal(ks[6], (DEPTH, D_MODEL), f32),
        'w_in': jax.random.normal(ks[7], (DEPTH, D_MODEL, D_PROJ), f32) * D_MODEL ** -0.5,
        'attn_sinks': 0.5 * jax.random.normal(ks[8], (DEPTH, N_HEADS), f32),
        'w_dw': jax.random.normal(ks[9], (DEPTH, CONV_WIDTH, D_CONV), f32) * CONV_WIDTH ** -0.5,
        'b_dw': 0.02 * jax.random.normal(ks[10], (DEPTH, D_CONV), f32),
        'conv_norm_gain': 1.0 + 0.02 * jax.random.normal(ks[11], (DEPTH, D_CONV), f32),
        'conv_norm_bias': 0.02 * jax.random.normal(ks[12], (DEPTH, D_CONV), f32),
        'w_out': jax.random.normal(ks[13], (DEPTH, D_MIX, D_MODEL), f32) * D_MIX ** -0.5,
        'final_norm_gain': 1.0 + 0.02 * jax.random.normal(ks[14], (D_MODEL,), f32),
    }


def reference(x_prompt, x_sample, cache_k_win, cache_v_win, state_conv, meta_tokens, norm_gain, w_in,
              attn_sinks, w_dw, b_dw, conv_norm_gain, conv_norm_bias, w_out, final_norm_gain):
    B = x_prompt.shape[0]
    meta = jnp.broadcast_to(meta_tokens[None].astype(x_prompt.dtype), (B, N_META, D_MODEL))
    hp = jnp.concatenate([meta, x_prompt], axis=1)
    hs = x_sample
    pos_p = jnp.arange(hp.shape[1])
    pos_s = PAST_LEN + jnp.arange(hs.shape[1])
    kp_l, vp_l, cp_l, ks_l, vs_l, cs_l = [], [], [], [], [], []
    for l in range(DEPTH):
        q, k, v, ga, a, b, gc = project(hp, norm_gain[l], w_in[l])
        q = rope(q, pos_p)
        k = rope(k, pos_p)
        ao = prompt_window_attention(q, k, v, attn_sinks[l])
        prefix = jnp.zeros((hp.shape[0], CONV_WIDTH - 1, D_CONV), hp.dtype)
        co, cst = conv_branch(a, b, prefix, w_dw[l], b_dw[l], conv_norm_gain[l], conv_norm_bias[l])
        hp = hp + merge(ao, ga, co, gc, w_out[l])
        kp_l.append(k[:, -WINDOW:])
        vp_l.append(v[:, -WINDOW:])
        cp_l.append(cst)
        q, k, v, ga, a, b, gc = project(hs, norm_gain[l], w_in[l])
        q = rope(q, pos_s)
        k = rope(k, pos_s)
        ao, kw, vw = sample_window_attention(q, k, v, cache_k_win[l], cache_v_win[l], attn_sinks[l])
        co, cst = conv_branch(a, b, state_conv[l], w_dw[l], b_dw[l], conv_norm_gain[l], conv_norm_bias[l])
        hs = hs + merge(ao, ga, co, gc, w_out[l])
        ks_l.append(kw)
        vs_l.append(vw)
        cs_l.append(cst)
    y_prompt = rms_norm(hp[:, N_META:], final_norm_gain)
    y_sample = rms_norm(hs, final_norm_gain)
    return (y_prompt, y_sample, jnp.stack(kp_l), jnp.stack(vp_l), jnp.stack(cp_l),
            jnp.stack(ks_l), jnp.stack(vs_l), jnp.stack(cs_l))
```

```python
import functools

import jax
import jax.numpy as jnp
from jax import lax
from jax.experimental import pallas as pl
from jax.experimental.pallas import tpu as pltpu

D_MODEL = 2048
N_META = 16
D_ATTN = 1024
D_CONV = 1024
HEAD_DIM = 64
N_HEADS = 16
N_KV_HEADS = 2
KV_WIDTH = N_KV_HEADS * HEAD_DIM
WINDOW = 128
BLOCK = 128
ROPE_DIM = 16
ROPE_THETA = 500000.0
CONV_WIDTH = 31
PREFIX = CONV_WIDTH - 1
D_PROJ = 2 * D_ATTN + 2 * KV_WIDTH + 3 * D_CONV
EPS = 1e-5
PAST_LEN = 8192

Q0 = 0
K0 = D_ATTN
V0 = K0 + KV_WIDTH
GA0 = V0 + KV_WIDTH
A0 = GA0 + D_ATTN
B0 = A0 + D_CONV
GC0 = B0 + D_CONV

LANES = 128
SUBLANES = 8
CHUNKS = D_ATTN // LANES
CHUNKS_PER_KV = CHUNKS // N_KV_HEADS
PAD = 32
NEG = -1e30

TQ = 256
VMEM_LIMIT = 56 * 1024 * 1024

f32 = jnp.float32
bf16 = jnp.bfloat16


def _rms_norm(x, g):
    return x * lax.rsqrt(jnp.mean(x * x, axis=-1, keepdims=True) + EPS) * g


def _sigmoid(x):
    return 1.0 / (1.0 + jnp.exp(-x))


def _silu(x):
    return x * _sigmoid(x)


def _rope(x, cos, sin_prev, sin_next):
    outs = []
    for c in range(x.shape[1] // LANES):
        xc = x[:, c * LANES:(c + 1) * LANES]
        outs.append(xc * cos + pltpu.roll(xc, 8, 1) * sin_prev + pltpu.roll(xc, LANES - 8, 1) * sin_next)
    return outs[0] if len(outs) == 1 else jnp.concatenate(outs, axis=1)


def _half_variants(x):
    lo = lax.broadcasted_iota(jnp.int32, x.shape, 1) < HEAD_DIM
    xr = pltpu.roll(x, HEAD_DIM, 1)
    zero = jnp.zeros_like(x)
    return (jnp.where(lo, x, zero).astype(bf16), jnp.where(lo, zero, xr).astype(bf16),
            jnp.where(lo, xr, zero).astype(bf16), jnp.where(lo, zero, x).astype(bf16))


def _layer_norm(x, g, b):
    mu = jnp.mean(x, axis=-1, keepdims=True)
    xc = x - mu
    return xc * lax.rsqrt(jnp.mean(xc * xc, axis=-1, keepdims=True) + EPS) * g + b


def _rope_tables(pos):
    half = ROPE_DIM // 2
    inv = ROPE_THETA ** (-jnp.arange(0, ROPE_DIM, 2, dtype=f32) / ROPE_DIM)
    ang = pos.astype(f32)[:, None] * inv[None, :]
    cos, sin = jnp.cos(ang), jnp.sin(ang)
    t = pos.shape[0]
    ones = jnp.ones((t, HEAD_DIM - ROPE_DIM), f32)
    zeros = jnp.zeros((t, HEAD_DIM - ROPE_DIM), f32)
    zh = jnp.zeros((t, half), f32)
    cos64 = jnp.concatenate([cos, cos, ones], axis=1)
    prev64 = jnp.concatenate([zh, sin, zeros], axis=1)
    next64 = jnp.concatenate([-sin, zh, zeros], axis=1)
    rep = LANES // HEAD_DIM
    return jnp.tile(cos64, (1, rep)), jnp.tile(prev64, (1, rep)), jnp.tile(next64, (1, rep))


PROJ_BN = 768


def _proj_kernel(x_ref, meta_ref, g_ref, w_ref, p_ref, pm_ref, u_ref):
    n = x_ref.shape[0]

    @pl.when(pl.program_id(0) == 0)
    def _():
        u_ref[0:n, :] = _rms_norm(x_ref[...], g_ref[...]).astype(bf16)
        u_ref[n:n + N_META, :] = _rms_norm(meta_ref[...], g_ref[...]).astype(bf16)

    p = jnp.dot(u_ref[...], w_ref[...], preferred_element_type=f32)
    p_ref[...] = p[0:n]
    pm_ref[...] = p[n:n + N_META]


def _project_sample(x, meta, g, w_in):
    n = x.shape[0]
    once = pl.Buffered(1)
    return pl.pallas_call(
        _proj_kernel,
        grid=(D_PROJ // PROJ_BN,),
        in_specs=[
            pl.BlockSpec((n, D_MODEL), lambda j: (0, 0), pipeline_mode=once),
            pl.BlockSpec((N_META, D_MODEL), lambda j: (0, 0), pipeline_mode=once),
            pl.BlockSpec((1, D_MODEL), lambda j: (0, 0), pipeline_mode=once),
            pl.BlockSpec((D_MODEL, PROJ_BN), lambda j: (0, j)),
        ],
        out_specs=[
            pl.BlockSpec((n, PROJ_BN), lambda j: (0, j)),
            pl.BlockSpec((N_META, PROJ_BN), lambda j: (0, j)),
        ],
        out_shape=[jax.ShapeDtypeStruct((n, D_PROJ), f32), jax.ShapeDtypeStruct((N_META, D_PROJ), f32)],
        scratch_shapes=[pltpu.VMEM((n + N_META, D_MODEL), bf16)],
        compiler_params=pltpu.CompilerParams(dimension_semantics=("arbitrary",), vmem_limit_bytes=VMEM_LIMIT),
    )(x, meta, g, w_in)


def _conv_ln(glu_ref, n_rows, wdw_ref, bdw_ref, lng_ref, lnb_ref):
    outs = []
    for r0 in range(0, n_rows, SUBLANES):
        acc = jnp.zeros((SUBLANES, D_CONV), f32)
        for k in range(CONV_WIDTH):
            start = PAD - PREFIX + r0 + k
            acc = acc + wdw_ref[k:k + 1, :] * glu_ref[start:start + SUBLANES, :]
        outs.append(acc)
    y = jnp.concatenate(outs, axis=0) + bdw_ref[...]
    return _silu(_layer_norm(y, lng_ref[...], lnb_ref[...]))


def _band_bias(first_valid_key):
    row = lax.broadcasted_iota(jnp.int32, (BLOCK, 2 * BLOCK), 0)
    col = lax.broadcasted_iota(jnp.int32, (BLOCK, 2 * BLOCK), 1)
    valid = (col >= row) & (col <= row + WINDOW) & (col >= first_valid_key)
    return jnp.where(valid, 0.0, NEG).astype(f32)


def _attn_block(q_blk, kvar_ref, row0, bias, sinks_ref):
    chunks = [None] * CHUNKS
    for kv in range(N_KV_HEADS):
        qs = jnp.concatenate(
            [q_blk[:, (kv * CHUNKS_PER_KV + i) * LANES:(kv * CHUNKS_PER_KV + i + 1) * LANES]
             for i in range(CHUNKS_PER_KV)], axis=0)
        acc = None
        for par in range(2):
            kk = kvar_ref[kv * 2 + par, row0:row0 + 2 * BLOCK, :]
            vv = kvar_ref[4 + kv * 2 + par, row0:row0 + 2 * BLOCK, :]
            s = lax.dot_general(qs, kk, (((1,), (1,)), ((), ())), preferred_element_type=f32)
            s = (s.reshape(CHUNKS_PER_KV, BLOCK, 2 * BLOCK) + bias[None]).reshape(CHUNKS_PER_KV * BLOCK, 2 * BLOCK)
            sink = jnp.concatenate(
                [jnp.full((BLOCK, 1), sinks_ref[kv * 8 + 2 * i + par], f32) for i in range(CHUNKS_PER_KV)], axis=0)
            m = jnp.maximum(jnp.max(s, axis=1, keepdims=True), sink)
            e = jnp.exp(s - m)
            den = jnp.sum(e, axis=1, keepdims=True) + jnp.exp(sink - m)
            o = jnp.dot(e.astype(bf16), vv, preferred_element_type=f32) * (1.0 / den)
            acc = o if acc is None else acc + o
        for i in range(CHUNKS_PER_KV):
            chunks[kv * CHUNKS_PER_KV + i] = acc[i * BLOCK:(i + 1) * BLOCK]
    return jnp.concatenate(chunks, axis=1)


def _prompt_kernel(sinks_ref, x_ref, cos_ref, sp_ref, sn_ref, pm_ref, mcos_ref, msp_ref, msn_ref,
                   g_ref, win_ref, wdw_ref, bdw_ref, lng_ref, lnb_ref, wout_ref, gf_ref,
                   y_ref, nk_ref, nv_ref, nc_ref, kvar_ref, glu_ref):
    t = pl.program_id(1)
    nt = pl.num_programs(1)

    @pl.when(t == 0)
    def _():
        pm = pm_ref[...]
        km = _rope(pm[:, K0:K0 + KV_WIDTH], mcos_ref[...], msp_ref[...], msn_ref[...])
        vm = pm[:, V0:V0 + KV_WIDTH]
        kvar_ref[:, 0:BLOCK - N_META, :] = jnp.zeros((8, BLOCK - N_META, LANES), bf16)
        for i, var in enumerate(_half_variants(km) + _half_variants(vm)):
            kvar_ref[i, BLOCK - N_META:BLOCK, :] = var
        glu_ref[0:PAD - N_META, :] = jnp.zeros((PAD - N_META, D_CONV), f32)
        glu_ref[PAD - N_META:PAD, :] = pm[:, A0:A0 + D_CONV] * _sigmoid(pm[:, B0:B0 + D_CONV])

    x = x_ref[0]
    u = _rms_norm(x, g_ref[...]).astype(bf16)

    def proj(c0, width):
        return jnp.dot(u, win_ref[:, c0:c0 + width], preferred_element_type=f32)

    kv = proj(K0, 2 * KV_WIDTH)
    k = _rope(kv[:, 0:KV_WIDTH], cos_ref[...], sp_ref[...], sn_ref[...])
    v = kv[:, KV_WIDTH:2 * KV_WIDTH]
    for i, var in enumerate(_half_variants(k) + _half_variants(v)):
        kvar_ref[i, BLOCK:BLOCK + TQ, :] = var

    glu_ref[PAD:PAD + TQ, :] = proj(A0, D_CONV) * _sigmoid(proj(B0, D_CONV))
    z_conv = _conv_ln(glu_ref, TQ, wdw_ref, bdw_ref, lng_ref, lnb_ref) * _silu(proj(GC0, D_CONV))

    q = (_rope(proj(Q0, D_ATTN), cos_ref[...], sp_ref[...], sn_ref[...]) * (HEAD_DIM ** -0.5)).astype(bf16)
    attn = []
    for j in range(TQ // BLOCK):
        first_valid = jnp.where((t == 0) & (j == 0), BLOCK - N_META, 0) if j == 0 else 0
        attn.append(_attn_block(q[j * BLOCK:(j + 1) * BLOCK], kvar_ref, j * BLOCK, _band_bias(first_valid), sinks_ref))
    z_attn = jnp.concatenate(attn, axis=0) * _silu(proj(GA0, D_ATTN))

    z = jnp.concatenate([z_attn.astype(bf16), z_conv.astype(bf16)], axis=1)
    h = x + jnp.dot(z, wout_ref[...], preferred_element_type=f32)
    y_ref[0] = _rms_norm(h, gf_ref[...])

    @pl.when(t == nt - 1)
    def _():
        nk_ref[0] = k[TQ - WINDOW:TQ]
        nv_ref[0] = v[TQ - WINDOW:TQ]
        nc_ref[0] = glu_ref[PAD + TQ - PREFIX:PAD + TQ, :]

    kvar_ref[:, 0:BLOCK, :] = kvar_ref[:, TQ:TQ + BLOCK, :]
    glu_ref[0:PAD, :] = glu_ref[TQ:TQ + PAD, :]


def _prompt_call(sinks, x, tabs, p_meta, mtabs, g, w_in, w_dw, b_dw, ln_g, ln_b, w_out, gf):
    nb, seq, _ = x.shape
    nt = seq // TQ
    once = pl.Buffered(1)

    def const(shape):
        return pl.BlockSpec(shape, lambda b, t: (0,) * len(shape), pipeline_mode=once)

    tab_spec = pl.BlockSpec((TQ, LANES), lambda b, t: (t, 0))
    return pl.pallas_call(
        _prompt_kernel,
        grid=(nb, nt),
        in_specs=[
            pl.BlockSpec(memory_space=pltpu.SMEM),
            pl.BlockSpec((1, TQ, D_MODEL), lambda b, t: (b, t, 0)),
            tab_spec, tab_spec, tab_spec,
            const((N_META, D_PROJ)), const((N_META, LANES)), const((N_META, LANES)), const((N_META, LANES)),
            const((1, D_MODEL)), const((D_MODEL, D_PROJ)),
            const((CONV_WIDTH, D_CONV)), const((1, D_CONV)), const((1, D_CONV)), const((1, D_CONV)),
            const((D_ATTN + D_CONV, D_MODEL)), const((1, D_MODEL)),
        ],
        out_specs=[
            pl.BlockSpec((1, TQ, D_MODEL), lambda b, t: (b, t, 0)),
            pl.BlockSpec((1, WINDOW, KV_WIDTH), lambda b, t: (b, 0, 0)),
            pl.BlockSpec((1, WINDOW, KV_WIDTH), lambda b, t: (b, 0, 0)),
            pl.BlockSpec((1, PREFIX, D_CONV), lambda b, t: (b, 0, 0)),
        ],
        out_shape=[
            jax.ShapeDtypeStruct((nb, seq, D_MODEL), f32),
            jax.ShapeDtypeStruct((nb, WINDOW, KV_WIDTH), f32),
            jax.ShapeDtypeStruct((nb, WINDOW, KV_WIDTH), f32),
            jax.ShapeDtypeStruct((nb, PREFIX, D_CONV), f32),
        ],
        scratch_shapes=[
            pltpu.VMEM((8, BLOCK + TQ, LANES), bf16),
            pltpu.VMEM((PAD + TQ, D_CONV), f32),
        ],
        compiler_params=pltpu.CompilerParams(
            dimension_semantics=("arbitrary", "arbitrary"), vmem_limit_bytes=VMEM_LIMIT),
    )(sinks, x, *tabs, p_meta, *mtabs, g, w_in, w_dw, b_dw, ln_g, ln_b, w_out, gf)


SEQ_BLOCK = 16
DEC = 8
KEYS = 2 * WINDOW


def _sample_mixer_kernel(p_ref, ck_ref, cv_ref, st_ref, cos_ref, sp_ref, sn_ref, sink_ref,
                         wdw_ref, bdw_ref, lng_ref, lnb_ref,
                         z_ref, nk_ref, nv_ref, nc_ref, q_scr, k_scr, v_scr, attn_scr, cbuf):
    rows = SEQ_BLOCK * DEC
    cos, sp, sn = cos_ref[...], sp_ref[...], sn_ref[...]

    k = _rope(p_ref[:, K0:K0 + KV_WIDTH], cos, sp, sn)
    v = p_ref[:, V0:V0 + KV_WIDTH]
    q_scr[...] = _rope(p_ref[:, Q0:Q0 + D_ATTN], cos, sp, sn) * (HEAD_DIM ** -0.5)
    k_scr[...] = k
    v_scr[...] = v

    nk_ref[:, 0:WINDOW - DEC, :] = ck_ref[:, DEC:WINDOW, :]
    nk_ref[:, WINDOW - DEC:WINDOW, :] = k.reshape(SEQ_BLOCK, DEC, KV_WIDTH)
    nv_ref[:, 0:WINDOW - DEC, :] = cv_ref[:, DEC:WINDOW, :]
    nv_ref[:, WINDOW - DEC:WINDOW, :] = v.reshape(SEQ_BLOCK, DEC, KV_WIDTH)

    glu = p_ref[:, A0:A0 + D_CONV] * _sigmoid(p_ref[:, B0:B0 + D_CONV])
    cbuf[:, 0:PREFIX, :] = st_ref[...]
    cbuf[:, PREFIX:PREFIX + DEC, :] = glu.reshape(SEQ_BLOCK, DEC, D_CONV)
    nc_ref[...] = cbuf[:, DEC:DEC + PREFIX, :]
    acc = jnp.zeros((SEQ_BLOCK, DEC, D_CONV), f32)
    for kk in range(CONV_WIDTH):
        acc = acc + wdw_ref[kk:kk + 1, :][None] * cbuf[:, kk:kk + DEC, :]
    yc = acc.reshape(rows, D_CONV) + bdw_ref[...]
    z_conv = _silu(_layer_norm(yc, lng_ref[...], lnb_ref[...])) * _silu(p_ref[:, GC0:GC0 + D_CONV])

    row = lax.broadcasted_iota(jnp.int32, (N_HEADS * DEC, KEYS), 0) % DEC
    col = lax.broadcasted_iota(jnp.int32, (N_HEADS * DEC, KEYS), 1)
    valid = ((col < WINDOW) & (col >= row)) | ((col >= WINDOW) & (col - WINDOW <= row))
    bias = jnp.where(valid, 0.0, NEG).astype(f32)
    lo = lax.broadcasted_iota(jnp.int32, (DEC, LANES), 1) < HEAD_DIM
    zero = jnp.zeros((DEC, LANES), f32)
    pad_rows = jnp.zeros((KEYS - WINDOW - DEC, LANES), f32)
    sink = sink_ref[...]

    def body(s, carry):
        r0 = pl.multiple_of(s * DEC, DEC)
        qs = q_scr[pl.ds(r0, DEC), :]
        pieces = [None] * N_HEADS
        for c in range(CHUNKS):
            qc = qs[:, c * LANES:(c + 1) * LANES]
            qr = pltpu.roll(qc, HEAD_DIM, 1)
            if c < CHUNKS_PER_KV:
                pieces[2 * c], pieces[2 * c + 1] = jnp.where(lo, qc, zero), jnp.where(lo, qr, zero)
            else:
                pieces[2 * c], pieces[2 * c + 1] = jnp.where(lo, zero, qr), jnp.where(lo, zero, qc)
        qrows = jnp.concatenate(pieces, axis=0).astype(bf16)
        kf = jnp.concatenate([ck_ref[s], k_scr[pl.ds(r0, DEC), :], pad_rows], axis=0).astype(bf16)
        vf = jnp.concatenate([cv_ref[s], v_scr[pl.ds(r0, DEC), :], pad_rows], axis=0).astype(bf16)
        sc = lax.dot_general(qrows, kf, (((1,), (1,)), ((), ())), preferred_element_type=f32) + bias
        m = jnp.maximum(jnp.max(sc, axis=1, keepdims=True), sink)
        e = jnp.exp(sc - m)
        den = jnp.sum(e, axis=1, keepdims=True) + jnp.exp(sink - m)
        o = jnp.dot(e.astype(bf16), vf, preferred_element_type=f32) * (1.0 / den)
        outs = []
        for c in range(CHUNKS):
            ev, od = o[2 * c * DEC:(2 * c + 1) * DEC], o[(2 * c + 1) * DEC:(2 * c + 2) * DEC]
            if c < CHUNKS_PER_KV:
                outs.append(jnp.where(lo, ev, pltpu.roll(od, HEAD_DIM, 1)))
            else:
                outs.append(jnp.where(lo, pltpu.roll(ev, HEAD_DIM, 1), od))
        attn_scr[pl.ds(r0, DEC), :] = jnp.concatenate(outs, axis=1)
        return carry

    lax.fori_loop(0, SEQ_BLOCK, body, 0)

    z_attn = attn_scr[...] * _silu(p_ref[:, GA0:GA0 + D_ATTN])
    z_ref[:, 0:D_ATTN] = z_attn.astype(bf16)
    z_ref[:, D_ATTN:D_ATTN + D_CONV] = z_conv.astype(bf16)


def _sample_mixer(p, ck, cv, st, tabs, sink_col, w_dw, b_dw, ln_g, ln_b):
    nseq = ck.shape[0]
    rows = SEQ_BLOCK * DEC

    def const(shape):
        return pl.BlockSpec(shape, lambda i: (0,) * len(shape))

    return pl.pallas_call(
        _sample_mixer_kernel,
        grid=(nseq // SEQ_BLOCK,),
        in_specs=[
            pl.BlockSpec((rows, D_PROJ), lambda i: (i, 0)),
            pl.BlockSpec((SEQ_BLOCK, WINDOW, KV_WIDTH), lambda i: (i, 0, 0)),
            pl.BlockSpec((SEQ_BLOCK, WINDOW, KV_WIDTH), lambda i: (i, 0, 0)),
            pl.BlockSpec((SEQ_BLOCK, PREFIX, D_CONV), lambda i: (i, 0, 0)),
            const((rows, LANES)), const((rows, LANES)), const((rows, LANES)),
            const((N_HEADS * DEC, 1)),
            const((CONV_WIDTH, D_CONV)), const((1, D_CONV)), const((1, D_CONV)), const((1, D_CONV)),
        ],
        out_specs=[
            pl.BlockSpec((rows, D_ATTN + D_CONV), lambda i: (i, 0)),
            pl.BlockSpec((SEQ_BLOCK, WINDOW, KV_WIDTH), lambda i: (i, 0, 0)),
            pl.BlockSpec((SEQ_BLOCK, WINDOW, KV_WIDTH), lambda i: (i, 0, 0)),
            pl.BlockSpec((SEQ_BLOCK, PREFIX, D_CONV), lambda i: (i, 0, 0)),
        ],
        out_shape=[
            jax.ShapeDtypeStruct((nseq * DEC, D_ATTN + D_CONV), bf16),
            jax.ShapeDtypeStruct((nseq, WINDOW, KV_WIDTH), f32),
            jax.ShapeDtypeStruct((nseq, WINDOW, KV_WIDTH), f32),
            jax.ShapeDtypeStruct((nseq, PREFIX, D_CONV), f32),
        ],
        scratch_shapes=[
            pltpu.VMEM((rows, D_ATTN), f32),
            pltpu.VMEM((rows, KV_WIDTH), f32),
            pltpu.VMEM((rows, KV_WIDTH), f32),
            pltpu.VMEM((rows, D_ATTN), f32),
            pltpu.VMEM((SEQ_BLOCK, PREFIX + DEC + 2, D_CONV), f32),
        ],
        compiler_params=pltpu.CompilerParams(dimension_semantics=("arbitrary",), vmem_limit_bytes=VMEM_LIMIT),
    )(p, ck, cv, st, *tabs, sink_col, w_dw, b_dw, ln_g, ln_b)


OUT_BM = 512


def _out_kernel(z_ref, x_ref, w_ref, gf_ref, y_ref):
    h = x_ref[...] + jnp.dot(z_ref[...], w_ref[...], preferred_element_type=f32)
    y_ref[...] = _rms_norm(h, gf_ref[...])


def _out_project(z, x, w_out, gf):
    n = x.shape[0]
    return pl.pallas_call(
        _out_kernel,
        grid=(n // OUT_BM,),
        in_specs=[
            pl.BlockSpec((OUT_BM, D_ATTN + D_CONV), lambda i: (i, 0)),
            pl.BlockSpec((OUT_BM, D_MODEL), lambda i: (i, 0)),
            pl.BlockSpec((D_ATTN + D_CONV, D_MODEL), lambda i: (0, 0), pipeline_mode=pl.Buffered(1)),
            pl.BlockSpec((1, D_MODEL), lambda i: (0, 0)),
        ],
        out_specs=pl.BlockSpec((OUT_BM, D_MODEL), lambda i: (i, 0)),
        out_shape=jax.ShapeDtypeStruct((n, D_MODEL), f32),
        compiler_params=pltpu.CompilerParams(dimension_semantics=("arbitrary",), vmem_limit_bytes=VMEM_LIMIT),
    )(z, x, w_out, gf)


def kernel(x_prompt, x_sample, cache_k_win, cache_v_win, state_conv, meta_tokens, norm_gain, w_in,
           attn_sinks, w_dw, b_dw, conv_norm_gain, conv_norm_bias, w_out, final_norm_gain):
    nb, seq, _ = x_prompt.shape
    nseq, dec, _ = x_sample.shape
    assert norm_gain.shape[0] == 1 and dec == DEC and seq % TQ == 0 and nseq % SEQ_BLOCK == 0

    g = norm_gain[0][None]
    w_in_b = w_in[0].astype(bf16)
    w_out_b = w_out[0].astype(bf16)
    sinks = attn_sinks[0]
    wdw, bdw = w_dw[0], b_dw[0][None]
    lng, lnb = conv_norm_gain[0][None], conv_norm_bias[0][None]
    gf = final_norm_gain[None]

    xs = x_sample.reshape(nseq * dec, D_MODEL)
    p_s, p_meta = _project_sample(xs, meta_tokens, g, w_in_b)

    tabs = _rope_tables(N_META + jnp.arange(seq))
    mtabs = _rope_tables(jnp.arange(N_META))
    y_prompt, nk_p, nv_p, nc_p = _prompt_call(sinks, x_prompt, tabs, p_meta, mtabs, g, w_in_b,
                                               wdw, bdw, lng, lnb, w_out_b, gf)

    stabs = _rope_tables(PAST_LEN + jnp.arange(SEQ_BLOCK * dec) % dec)
    sink_col = jnp.repeat(sinks, dec)[:, None]
    ck = cache_k_win[0].reshape(nseq, WINDOW, KV_WIDTH)
    cv = cache_v_win[0].reshape(nseq, WINDOW, KV_WIDTH)
    z_s, nk_s, nv_s, nc_s = _sample_mixer(p_s, ck, cv, state_conv[0], stabs, sink_col, wdw, bdw, lng, lnb)
    y_sample = _out_project(z_s, xs, w_out_b, gf).reshape(nseq, dec, D_MODEL)

    kv_shape = (1, -1, WINDOW, N_KV_HEADS, HEAD_DIM)
    return (y_prompt, y_sample, nk_p.reshape(kv_shape), nv_p.reshape(kv_shape), nc_p[None],
            nk_s.reshape(kv_shape), nv_s.reshape(kv_shape), nc_s[None])
```

```python
import functools

import jax
import jax.numpy as jnp
from jax import lax
from jax.experimental import pallas as pl
from jax.experimental.pallas import tpu as pltpu

D_MODEL = 2048
N_META = 16
D_ATTN = 1024
D_CONV = 1024
HEAD_DIM = 64
N_HEADS = 16
N_KV_HEADS = 2
KV_WIDTH = N_KV_HEADS * HEAD_DIM
WINDOW = 128
BLOCK = 128
ROPE_DIM = 16
ROPE_THETA = 500000.0
CONV_WIDTH = 31
PREFIX = CONV_WIDTH - 1
D_PROJ = 2 * D_ATTN + 2 * KV_WIDTH + 3 * D_CONV
EPS = 1e-5
PAST_LEN = 8192

Q0 = 0
K0 = D_ATTN
V0 = K0 + KV_WIDTH
GA0 = V0 + KV_WIDTH
A0 = GA0 + D_ATTN
B0 = A0 + D_CONV
GC0 = B0 + D_CONV

LANES = 128
SUBLANES = 8
CHUNKS = D_ATTN // LANES
CHUNKS_PER_KV = CHUNKS // N_KV_HEADS
PAD = 32
NEG = -1e30

TQ = 256
VMEM_LIMIT = 56 * 1024 * 1024

f32 = jnp.float32
bf16 = jnp.bfloat16


def _rms_norm(x, g):
    return x * lax.rsqrt(jnp.mean(x * x, axis=-1, keepdims=True) + EPS) * g


def _sigmoid(x):
    return 1.0 / (1.0 + jnp.exp(-x))


def _silu(x):
    return x * _sigmoid(x)


def _rope(x, cos, sin_prev, sin_next):
    outs = []
    for c in range(x.shape[1] // LANES):
        xc = x[:, c * LANES:(c + 1) * LANES]
        outs.append(xc * cos + pltpu.roll(xc, 8, 1) * sin_prev + pltpu.roll(xc, LANES - 8, 1) * sin_next)
    return outs[0] if len(outs) == 1 else jnp.concatenate(outs, axis=1)


def _half_variants(x):
    lo = lax.broadcasted_iota(jnp.int32, x.shape, 1) < HEAD_DIM
    xr = pltpu.roll(x, HEAD_DIM, 1)
    zero = jnp.zeros_like(x)
    return (jnp.where(lo, x, zero).astype(bf16), jnp.where(lo, zero, xr).astype(bf16),
            jnp.where(lo, xr, zero).astype(bf16), jnp.where(lo, zero, x).astype(bf16))


def _value_variants_t(v):
    vt = v.T
    top, bot = vt[0:HEAD_DIM], vt[HEAD_DIM:2 * HEAD_DIM]
    zero = jnp.zeros_like(top)
    return tuple(jnp.concatenate(pair, axis=0).astype(bf16)
                 for pair in ((top, zero), (zero, top), (bot, zero), (zero, bot)))


def _layer_norm(x, g, b):
    mu = jnp.mean(x, axis=-1, keepdims=True)
    xc = x - mu
    return xc * lax.rsqrt(jnp.mean(xc * xc, axis=-1, keepdims=True) + EPS) * g + b


def _rope_tables(pos):
    half = ROPE_DIM // 2
    inv = ROPE_THETA ** (-jnp.arange(0, ROPE_DIM, 2, dtype=f32) / ROPE_DIM)
    ang = pos.astype(f32)[:, None] * inv[None, :]
    cos, sin = jnp.cos(ang), jnp.sin(ang)
    t = pos.shape[0]
    ones = jnp.ones((t, HEAD_DIM - ROPE_DIM), f32)
    zeros = jnp.zeros((t, HEAD_DIM - ROPE_DIM), f32)
    zh = jnp.zeros((t, half), f32)
    cos64 = jnp.concatenate([cos, cos, ones], axis=1)
    prev64 = jnp.concatenate([zh, sin, zeros], axis=1)
    next64 = jnp.concatenate([-sin, zh, zeros], axis=1)
    rep = LANES // HEAD_DIM
    return jnp.tile(cos64, (1, rep)), jnp.tile(prev64, (1, rep)), jnp.tile(next64, (1, rep))


PROJ_BN = 768


def _proj_kernel(x_ref, meta_ref, g_ref, w_ref, p_ref, pm_ref, u_ref):
    n = x_ref.shape[0]

    @pl.when(pl.program_id(0) == 0)
    def _():
        u_ref[0:n, :] = _rms_norm(x_ref[...], g_ref[...]).astype(bf16)
        u_ref[n:n + N_META, :] = _rms_norm(meta_ref[...], g_ref[...]).astype(bf16)

    p = jnp.dot(u_ref[...], w_ref[...], preferred_element_type=f32)
    p_ref[...] = p[0:n]
    pm_ref[...] = p[n:n + N_META]


def _project_sample(x, meta, g, w_in):
    n = x.shape[0]
    once = pl.Buffered(1)
    return pl.pallas_call(
        _proj_kernel,
        grid=(D_PROJ // PROJ_BN,),
        in_specs=[
            pl.BlockSpec((n, D_MODEL), lambda j: (0, 0), pipeline_mode=once),
            pl.BlockSpec((N_META, D_MODEL), lambda j: (0, 0), pipeline_mode=once),
            pl.BlockSpec((1, D_MODEL), lambda j: (0, 0), pipeline_mode=once),
            pl.BlockSpec((D_MODEL, PROJ_BN), lambda j: (0, j)),
        ],
        out_specs=[
            pl.BlockSpec((n, PROJ_BN), lambda j: (0, j)),
            pl.BlockSpec((N_META, PROJ_BN), lambda j: (0, j)),
        ],
        out_shape=[jax.ShapeDtypeStruct((n, D_PROJ), f32), jax.ShapeDtypeStruct((N_META, D_PROJ), f32)],
        scratch_shapes=[pltpu.VMEM((n + N_META, D_MODEL), bf16)],
        compiler_params=pltpu.CompilerParams(dimension_semantics=("arbitrary",), vmem_limit_bytes=VMEM_LIMIT),
    )(x, meta, g, w_in)


CONV_OFF = PAD - PREFIX
CONV_TAP_TILES = (CONV_OFF + CONV_WIDTH - 1) // SUBLANES + 1


def _conv_chunk(load_tile, n_seq, n_tiles, wdw_ref, c):
    row = lax.broadcasted_iota(jnp.int32, (SUBLANES, LANES), 0)
    lanes = slice(c * LANES, (c + 1) * LANES)
    w = [jnp.broadcast_to(wdw_ref[k:k + 1, lanes], (SUBLANES, LANES)) for k in range(CONV_WIDTH)]
    outs = []
    for s in range(n_seq):
        g = [load_tile(s, j, lanes) for j in range(n_tiles + CONV_TAP_TILES - 1)]

        def z(r, i):
            acc = None
            for a in range(CONV_TAP_TILES):
                k = SUBLANES * a + r - CONV_OFF
                if 0 <= k < CONV_WIDTH:
                    term = w[k] * g[i + a]
                    acc = term if acc is None else acc + term
            return acc

        prev = [z(r, 0) for r in range(1, SUBLANES)]
        for i in range(n_tiles):
            nxt = [z(r, i + 1) for r in range(1, SUBLANES)]
            y = z(0, i)
            for r in range(1, SUBLANES):
                y = y + pltpu.roll(jnp.where(row >= r, prev[r - 1], nxt[r - 1]), SUBLANES - r, 0)
            outs.append(y)
            prev = nxt
    return jnp.concatenate(outs, axis=0)


def _causal_conv(load_tile, n_seq, n_tiles, wdw_ref):
    return jnp.concatenate([_conv_chunk(load_tile, n_seq, n_tiles, wdw_ref, c)
                            for c in range(D_CONV // LANES)], axis=1)


def _conv_ln(y, bdw_ref, lng_ref, lnb_ref):
    return _silu(_layer_norm(y + bdw_ref[...], lng_ref[...], lnb_ref[...]))


def _band_bias_t(first_valid_key):
    key = lax.broadcasted_iota(jnp.int32, (2 * BLOCK, BLOCK), 0)
    qry = lax.broadcasted_iota(jnp.int32, (2 * BLOCK, BLOCK), 1)
    valid = (key >= qry) & (key <= qry + WINDOW) & (key >= first_valid_key)
    return jnp.where(valid, 0.0, NEG).astype(f32)


def _attn_piece(q_blk, kvar_ref, vt_ref, row0, bias_t, sinks_ref, kv, par):
    qs = jnp.concatenate(
        [q_blk[:, (kv * CHUNKS_PER_KV + i) * LANES:(kv * CHUNKS_PER_KV + i + 1) * LANES]
         for i in range(CHUNKS_PER_KV)], axis=0)
    kk = kvar_ref[kv * 2 + par, row0:row0 + 2 * BLOCK, :]
    vt = vt_ref[kv * 2 + par, :, row0:row0 + 2 * BLOCK]
    st = lax.dot_general(kk, qs, (((1,), (1,)), ((), ())), preferred_element_type=f32)
    es, invs = [], []
    for i in range(CHUNKS_PER_KV):
        blk = st[:, i * BLOCK:(i + 1) * BLOCK] + bias_t
        sink = sinks_ref[kv * 8 + 2 * i + par]
        m = jnp.maximum(jnp.max(blk, axis=0, keepdims=True), sink)
        e = jnp.exp(blk - m)
        invs.append(1.0 / (jnp.sum(e, axis=0, keepdims=True) + jnp.exp(sink - m)))
        es.append(e.astype(bf16))
    out_t = jnp.dot(vt, jnp.concatenate(es, axis=1), preferred_element_type=f32)
    return out_t * jnp.concatenate(invs, axis=1)


PROJ_PIECE = 512
OUT_PIECE = 512
GATE_ROWS = 16


def _token(x):
    t = x.reshape(-1, SUBLANES, x.shape[-1]).sum(axis=0)
    parts = [t[:, c:c + LANES] for c in range(0, x.shape[-1], LANES)]
    while len(parts) > 1:
        parts = [a + b for a, b in zip(parts[0::2], parts[1::2])] + parts[len(parts) & ~1:]
    return jnp.concatenate([parts[0], parts[0]], axis=0).astype(bf16)


def _lockstep(matmul_stage, n_matmul, vector_stage, n_vector, lhs_ref):
    spare = lhs_ref.shape[0] - GATE_ROWS
    next(matmul_stage)
    done, credit = 1, 0.0
    for tok in vector_stage:
        lhs_ref[spare:spare + GATE_ROWS, 0:LANES] = tok
        credit += (n_matmul - 1) / n_vector
        while credit >= 1.0 and done < n_matmul:
            next(matmul_stage)
            done, credit = done + 1, credit - 1.0
    for _ in matmul_stage:
        pass


def _prompt_kernel(anchor_ref, sinks_ref, x_ref, cos_ref, sp_ref, sn_ref, pm_ref, mcos_ref, msp_ref, msn_ref,
                   g_ref, win_ref, wdw_ref, bdw_ref, lng_ref, lnb_ref, wout_ref, gf_ref,
                   y_ref, nk_ref, nv_ref, nc_ref, kvar_ref, vt_ref, glu_ref, u_ref, z_ref):
    t = pl.program_id(1)
    nt = pl.num_programs(1)

    @pl.when(t == 0)
    def _():
        pm = pm_ref[...]
        km = _rope(pm[:, K0:K0 + KV_WIDTH], mcos_ref[...], msp_ref[...], msn_ref[...])
        vm = pm[:, V0:V0 + KV_WIDTH]
        invalid = jnp.zeros((BLOCK - N_META, LANES), f32)
        kvar_ref[:, 0:BLOCK - N_META, :] = jnp.zeros((4, BLOCK - N_META, LANES), bf16)
        for i, var in enumerate(_half_variants(km)):
            kvar_ref[i, BLOCK - N_META:BLOCK, :] = var
        for i, var in enumerate(_value_variants_t(jnp.concatenate([invalid, vm], axis=0))):
            vt_ref[i, :, 0:BLOCK] = var
        glu_ref[0:PAD - N_META, :] = jnp.zeros((PAD - N_META, D_CONV), f32)
        glu_ref[PAD - N_META:PAD, :] = pm[:, A0:A0 + D_CONV] * _sigmoid(pm[:, B0:B0 + D_CONV])

    zero = anchor_ref[0] * GATE_ROWS

    def lhs_rows(ref, j):
        return ref[pl.ds(pl.multiple_of(zero + j * BLOCK, GATE_ROWS), BLOCK), :]

    def project(j, st):
        rows = slice(j * BLOCK, (j + 1) * BLOCK)
        u_ref[rows, :] = _rms_norm(x_ref[0, rows, :], g_ref[...]).astype(bf16)
        cos, sp, sn = cos_ref[rows, :], sp_ref[rows, :], sn_ref[rows, :]

        def cols(c0, width):
            parts = []
            for c in range(c0, c0 + width, PROJ_PIECE):
                n = min(PROJ_PIECE, c0 + width - c)
                parts.append(jnp.dot(lhs_rows(u_ref, j), win_ref[:, c:c + n], preferred_element_type=f32))
                yield
            return parts[0] if len(parts) == 1 else jnp.concatenate(parts, axis=1)

        kv = yield from cols(K0, 2 * KV_WIDTH)
        st["k"] = k = _rope(kv[:, 0:KV_WIDTH], cos, sp, sn)
        st["v"] = v = kv[:, KV_WIDTH:2 * KV_WIDTH]
        for i, var in enumerate(_half_variants(k)):
            kvar_ref[i, (j + 1) * BLOCK:(j + 2) * BLOCK, :] = var
        for i, var in enumerate(_value_variants_t(v)):
            vt_ref[i, :, (j + 1) * BLOCK:(j + 2) * BLOCK] = var
        a = yield from cols(A0, D_CONV)
        b = yield from cols(B0, D_CONV)
        glu_ref[PAD + j * BLOCK:PAD + (j + 1) * BLOCK, :] = a * _sigmoid(b)
        q = yield from cols(Q0, D_ATTN)
        st["q"] = (_rope(q, cos, sp, sn) * (HEAD_DIM ** -0.5)).astype(bf16)
        st["gc"] = yield from cols(GC0, D_CONV)
        st["ga"] = yield from cols(GA0, D_ATTN)

    n_project = 1 + 5 * (D_CONV // PROJ_PIECE)

    def mix(j, st):
        rows = slice(j * BLOCK, (j + 1) * BLOCK)
        tile0 = j * BLOCK // SUBLANES

        def load_tile(s, i, lanes):
            return glu_ref[(tile0 + i) * SUBLANES:(tile0 + i + 1) * SUBLANES, lanes]

        conv = []
        for c in range(D_CONV // LANES):
            conv.append(_conv_chunk(load_tile, 1, BLOCK // SUBLANES, wdw_ref, c))
            yield _token(conv[-1])
        z_conv = _conv_ln(jnp.concatenate(conv, axis=1), bdw_ref, lng_ref, lnb_ref) * _silu(st["gc"])
        z_ref[rows, D_ATTN:D_ATTN + D_CONV] = z_conv.astype(bf16)
        yield _token(z_conv)
        first_valid = jnp.where(t == 0, BLOCK - N_META, 0) if j == 0 else 0
        bias_t = _band_bias_t(first_valid)
        chunks = [None] * CHUNKS
        for kv in range(N_KV_HEADS):
            acc = (_attn_piece(st["q"], kvar_ref, vt_ref, j * BLOCK, bias_t, sinks_ref, kv, 0)
                   + _attn_piece(st["q"], kvar_ref, vt_ref, j * BLOCK, bias_t, sinks_ref, kv, 1))
            for i in range(CHUNKS_PER_KV):
                chunks[kv * CHUNKS_PER_KV + i] = acc[:, i * BLOCK:(i + 1) * BLOCK].T
        z_attn = jnp.concatenate(chunks, axis=1) * _silu(st["ga"])
        z_ref[rows, 0:D_ATTN] = z_attn.astype(bf16)
        yield _token(z_attn)

    n_mix = D_CONV // LANES + 2

    def out_project(j):
        rows = slice(j * BLOCK, (j + 1) * BLOCK)
        hs, ssq = [], None
        for c in range(0, D_MODEL, OUT_PIECE):
            h = x_ref[0, rows, c:c + OUT_PIECE] + jnp.dot(lhs_rows(z_ref, j), wout_ref[:, c:c + OUT_PIECE],
                                                          preferred_element_type=f32)
            part = jnp.sum(h * h, axis=-1, keepdims=True)
            ssq = part if ssq is None else ssq + part
            hs.append(h)
            yield
        scale = lax.rsqrt(ssq * (1.0 / D_MODEL) + EPS)
        for i, h in enumerate(hs):
            y_ref[0, rows, i * OUT_PIECE:(i + 1) * OUT_PIECE] = h * scale * gf_ref[:, i * OUT_PIECE:(i + 1) * OUT_PIECE]

    n_out = D_MODEL // OUT_PIECE

    assert TQ == 2 * BLOCK
    st0, st1 = {}, {}
    for _ in project(0, st0):
        pass
    _lockstep(project(1, st1), n_project, mix(0, st0), n_mix, u_ref)
    _lockstep(out_project(0), n_out, mix(1, st1), n_mix, z_ref)
    for _ in out_project(1):
        pass

    @pl.when(t == nt - 1)
    def _():
        nk_ref[0] = st1["k"]
        nv_ref[0] = st1["v"]
        nc_ref[0] = glu_ref[PAD + TQ - PREFIX:PAD + TQ, :]

    kvar_ref[:, 0:BLOCK, :] = kvar_ref[:, TQ:TQ + BLOCK, :]
    vt_ref[:, :, 0:BLOCK] = vt_ref[:, :, TQ:TQ + BLOCK]
    glu_ref[0:PAD, :] = glu_ref[TQ:TQ + PAD, :]


def _prompt_call(sinks, x, tabs, p_meta, mtabs, g, w_in, w_dw, b_dw, ln_g, ln_b, w_out, gf):
    nb, seq, _ = x.shape
    nt = seq // TQ
    once = pl.Buffered(1)

    def const(shape):
        return pl.BlockSpec(shape, lambda b, t: (0,) * len(shape), pipeline_mode=once)

    tab_spec = pl.BlockSpec((TQ, LANES), lambda b, t: (t, 0))
    return pl.pallas_call(
        _prompt_kernel,
        grid=(nb, nt),
        in_specs=[
            pl.BlockSpec(memory_space=pltpu.SMEM),
            pl.BlockSpec(memory_space=pltpu.SMEM),
            pl.BlockSpec((1, TQ, D_MODEL), lambda b, t: (b, t, 0)),
            tab_spec, tab_spec, tab_spec,
            const((N_META, D_PROJ)), const((N_META, LANES)), const((N_META, LANES)), const((N_META, LANES)),
            const((1, D_MODEL)), const((D_MODEL, D_PROJ)),
            const((CONV_WIDTH, D_CONV)), const((1, D_CONV)), const((1, D_CONV)), const((1, D_CONV)),
            const((D_ATTN + D_CONV, D_MODEL)), const((1, D_MODEL)),
        ],
        out_specs=[
            pl.BlockSpec((1, TQ, D_MODEL), lambda b, t: (b, t, 0)),
            pl.BlockSpec((1, WINDOW, KV_WIDTH), lambda b, t: (b, 0, 0)),
            pl.BlockSpec((1, WINDOW, KV_WIDTH), lambda b, t: (b, 0, 0)),
            pl.BlockSpec((1, PREFIX, D_CONV), lambda b, t: (b, 0, 0)),
        ],
        out_shape=[
            jax.ShapeDtypeStruct((nb, seq, D_MODEL), f32),
            jax.ShapeDtypeStruct((nb, WINDOW, KV_WIDTH), f32),
            jax.ShapeDtypeStruct((nb, WINDOW, KV_WIDTH), f32),
            jax.ShapeDtypeStruct((nb, PREFIX, D_CONV), f32),
        ],
        scratch_shapes=[
            pltpu.VMEM((4, BLOCK + TQ, LANES), bf16),
            pltpu.VMEM((4, LANES, BLOCK + TQ), bf16),
            pltpu.VMEM((PAD + TQ, D_CONV), f32),
            pltpu.VMEM((TQ + GATE_ROWS, D_MODEL), bf16),
            pltpu.VMEM((TQ + GATE_ROWS, D_ATTN + D_CONV), bf16),
        ],
        compiler_params=pltpu.CompilerParams(
            dimension_semantics=("arbitrary", "arbitrary"), vmem_limit_bytes=VMEM_LIMIT),
    )(jnp.zeros((1,), jnp.int32), sinks, x, *tabs, p_meta, *mtabs, g, w_in, w_dw, b_dw, ln_g, ln_b, w_out, gf)


SEQ_BLOCK = 16
SEQ_UNROLL = 4
DEC = 8
KEYS = 2 * WINDOW


def _sample_mixer_kernel(p_ref, ck_ref, cv_ref, st_ref, cos_ref, sp_ref, sn_ref, sink_ref,
                         wdw_ref, bdw_ref, lng_ref, lnb_ref,
                         z_ref, nk_ref, nv_ref, nc_ref, q_scr, k_scr, v_scr, attn_scr, cbuf):
    rows = SEQ_BLOCK * DEC
    cos, sp, sn = cos_ref[...], sp_ref[...], sn_ref[...]

    k = _rope(p_ref[:, K0:K0 + KV_WIDTH], cos, sp, sn)
    v = p_ref[:, V0:V0 + KV_WIDTH]
    q_scr[...] = _rope(p_ref[:, Q0:Q0 + D_ATTN], cos, sp, sn) * (HEAD_DIM ** -0.5)
    k_scr[...] = k
    v_scr[...] = v

    nk_ref[:, 0:WINDOW - DEC, :] = ck_ref[:, DEC:WINDOW, :]
    nk_ref[:, WINDOW - DEC:WINDOW, :] = k.reshape(SEQ_BLOCK, DEC, KV_WIDTH)
    nv_ref[:, 0:WINDOW - DEC, :] = cv_ref[:, DEC:WINDOW, :]
    nv_ref[:, WINDOW - DEC:WINDOW, :] = v.reshape(SEQ_BLOCK, DEC, KV_WIDTH)

    glu = p_ref[:, A0:A0 + D_CONV] * _sigmoid(p_ref[:, B0:B0 + D_CONV])
    cbuf[:, 0:SUBLANES, :] = jnp.zeros((SEQ_BLOCK, SUBLANES, D_CONV), f32)
    cbuf[:, CONV_OFF:PAD, :] = st_ref[...]
    cbuf[:, PAD:PAD + DEC, :] = glu.reshape(SEQ_BLOCK, DEC, D_CONV)
    nc_ref[...] = cbuf[:, PAD + DEC - PREFIX:PAD + DEC, :]
    conv = _causal_conv(lambda s, j, lanes: cbuf[s, j * SUBLANES:(j + 1) * SUBLANES, lanes],
                        SEQ_BLOCK, DEC // SUBLANES, wdw_ref)
    z_conv = _conv_ln(conv, bdw_ref, lng_ref, lnb_ref) * _silu(p_ref[:, GC0:GC0 + D_CONV])

    row = lax.broadcasted_iota(jnp.int32, (N_HEADS * DEC, KEYS), 0) % DEC
    col = lax.broadcasted_iota(jnp.int32, (N_HEADS * DEC, KEYS), 1)
    valid = ((col < WINDOW) & (col >= row)) | ((col >= WINDOW) & (col - WINDOW <= row))
    bias = jnp.where(valid, 0.0, NEG).astype(f32)
    lo = lax.broadcasted_iota(jnp.int32, (DEC, LANES), 1) < HEAD_DIM
    zero = jnp.zeros((DEC, LANES), f32)
    pad_rows = jnp.zeros((KEYS - WINDOW - DEC, LANES), f32)
    sink = sink_ref[...]

    def body(s, carry):
        r0 = pl.multiple_of(s * DEC, DEC)
        qs = q_scr[pl.ds(r0, DEC), :]
        pieces = [None] * N_HEADS
        for c in range(CHUNKS):
            qc = qs[:, c * LANES:(c + 1) * LANES]
            qr = pltpu.roll(qc, HEAD_DIM, 1)
            if c < CHUNKS_PER_KV:
                pieces[2 * c], pieces[2 * c + 1] = jnp.where(lo, qc, zero), jnp.where(lo, qr, zero)
            else:
                pieces[2 * c], pieces[2 * c + 1] = jnp.where(lo, zero, qr), jnp.where(lo, zero, qc)
        qrows = jnp.concatenate(pieces, axis=0).astype(bf16)
        kf = jnp.concatenate([ck_ref[s], k_scr[pl.ds(r0, DEC), :], pad_rows], axis=0).astype(bf16)
        vf = jnp.concatenate([cv_ref[s], v_scr[pl.ds(r0, DEC), :], pad_rows], axis=0).astype(bf16)
        sc = lax.dot_general(qrows, kf, (((1,), (1,)), ((), ())), preferred_element_type=f32) + bias
        m = jnp.maximum(jnp.max(sc, axis=1, keepdims=True), sink)
        e = jnp.exp(sc - m)
        den = jnp.sum(e, axis=1, keepdims=True) + jnp.exp(sink - m)
        o = jnp.dot(e.astype(bf16), vf, preferred_element_type=f32) * (1.0 / den)
        outs = []
        for c in range(CHUNKS):
            ev, od = o[2 * c * DEC:(2 * c + 1) * DEC], o[(2 * c + 1) * DEC:(2 * c + 2) * DEC]
            if c < CHUNKS_PER_KV:
                outs.append(jnp.where(lo, ev, pltpu.roll(od, HEAD_DIM, 1)))
            else:
                outs.append(jnp.where(lo, pltpu.roll(ev, HEAD_DIM, 1), od))
        attn_scr[pl.ds(r0, DEC), :] = jnp.concatenate(outs, axis=1)
        return carry

    lax.fori_loop(0, SEQ_BLOCK, body, 0, unroll=SEQ_UNROLL)

    z_attn = attn_scr[...] * _silu(p_ref[:, GA0:GA0 + D_ATTN])
    z_ref[:, 0:D_ATTN] = z_attn.astype(bf16)
    z_ref[:, D_ATTN:D_ATTN + D_CONV] = z_conv.astype(bf16)


def _sample_mixer(p, ck, cv, st, tabs, sink_col, w_dw, b_dw, ln_g, ln_b):
    nseq = ck.shape[0]
    rows = SEQ_BLOCK * DEC

    def const(shape):
        return pl.BlockSpec(shape, lambda i: (0,) * len(shape))

    return pl.pallas_call(
        _sample_mixer_kernel,
        grid=(nseq // SEQ_BLOCK,),
        in_specs=[
            pl.BlockSpec((rows, D_PROJ), lambda i: (i, 0)),
            pl.BlockSpec((SEQ_BLOCK, WINDOW, KV_WIDTH), lambda i: (i, 0, 0)),
            pl.BlockSpec((SEQ_BLOCK, WINDOW, KV_WIDTH), lambda i: (i, 0, 0)),
            pl.BlockSpec((SEQ_BLOCK, PREFIX, D_CONV), lambda i: (i, 0, 0)),
            const((rows, LANES)), const((rows, LANES)), const((rows, LANES)),
            const((N_HEADS * DEC, 1)),
            const((CONV_WIDTH, D_CONV)), const((1, D_CONV)), const((1, D_CONV)), const((1, D_CONV)),
        ],
        out_specs=[
            pl.BlockSpec((rows, D_ATTN + D_CONV), lambda i: (i, 0)),
            pl.BlockSpec((SEQ_BLOCK, WINDOW, KV_WIDTH), lambda i: (i, 0, 0)),
            pl.BlockSpec((SEQ_BLOCK, WINDOW, KV_WIDTH), lambda i: (i, 0, 0)),
            pl.BlockSpec((SEQ_BLOCK, PREFIX, D_CONV), lambda i: (i, 0, 0)),
        ],
        out_shape=[
            jax.ShapeDtypeStruct((nseq * DEC, D_ATTN + D_CONV), bf16),
            jax.ShapeDtypeStruct((nseq, WINDOW, KV_WIDTH), f32),
            jax.ShapeDtypeStruct((nseq, WINDOW, KV_WIDTH), f32),
            jax.ShapeDtypeStruct((nseq, PREFIX, D_CONV), f32),
        ],
        scratch_shapes=[
            pltpu.VMEM((rows, D_ATTN), f32),
            pltpu.VMEM((rows, KV_WIDTH), f32),
            pltpu.VMEM((rows, KV_WIDTH), f32),
            pltpu.VMEM((rows, D_ATTN), f32),
            pltpu.VMEM((SEQ_BLOCK, PAD + DEC, D_CONV), f32),
        ],
        compiler_params=pltpu.CompilerParams(dimension_semantics=("arbitrary",), vmem_limit_bytes=VMEM_LIMIT),
    )(p, ck, cv, st, *tabs, sink_col, w_dw, b_dw, ln_g, ln_b)


OUT_BM = 512


def _out_kernel(z_ref, x_ref, w_ref, gf_ref, y_ref):
    h = x_ref[...] + jnp.dot(z_ref[...], w_ref[...], preferred_element_type=f32)
    y_ref[...] = _rms_norm(h, gf_ref[...])


def _out_project(z, x, w_out, gf):
    n = x.shape[0]
    return pl.pallas_call(
        _out_kernel,
        grid=(n // OUT_BM,),
        in_specs=[
            pl.BlockSpec((OUT_BM, D_ATTN + D_CONV), lambda i: (i, 0)),
            pl.BlockSpec((OUT_BM, D_MODEL), lambda i: (i, 0)),
            pl.BlockSpec((D_ATTN + D_CONV, D_MODEL), lambda i: (0, 0), pipeline_mode=pl.Buffered(1)),
            pl.BlockSpec((1, D_MODEL), lambda i: (0, 0)),
        ],
        out_specs=pl.BlockSpec((OUT_BM, D_MODEL), lambda i: (i, 0)),
        out_shape=jax.ShapeDtypeStruct((n, D_MODEL), f32),
        compiler_params=pltpu.CompilerParams(dimension_semantics=("arbitrary",), vmem_limit_bytes=VMEM_LIMIT),
    )(z, x, w_out, gf)


def kernel(x_prompt, x_sample, cache_k_win, cache_v_win, state_conv, meta_tokens, norm_gain, w_in,
           attn_sinks, w_dw, b_dw, conv_norm_gain, conv_norm_bias, w_out, final_norm_gain):
    nb, seq, _ = x_prompt.shape
    nseq, dec, _ = x_sample.shape
    assert norm_gain.shape[0] == 1 and dec == DEC and seq % TQ == 0 and nseq % SEQ_BLOCK == 0

    g = norm_gain[0][None]
    w_in_b = w_in[0].astype(bf16)
    w_out_b = w_out[0].astype(bf16)
    sinks = attn_sinks[0]
    wdw, bdw = w_dw[0], b_dw[0][None]
    lng, lnb = conv_norm_gain[0][None], conv_norm_bias[0][None]
    gf = final_norm_gain[None]

    xs = x_sample.reshape(nseq * dec, D_MODEL)
    p_s, p_meta = _project_sample(xs, meta_tokens, g, w_in_b)

    tabs = _rope_tables(N_META + jnp.arange(seq))
    mtabs = _rope_tables(jnp.arange(N_META))
    y_prompt, nk_p, nv_p, nc_p = _prompt_call(sinks, x_prompt, tabs, p_meta, mtabs, g, w_in_b,
                                               wdw, bdw, lng, lnb, w_out_b, gf)

    stabs = _rope_tables(PAST_LEN + jnp.arange(SEQ_BLOCK * dec) % dec)
    sink_col = jnp.repeat(sinks, dec)[:, None]
    ck = cache_k_win[0].reshape(nseq, WINDOW, KV_WIDTH)
    cv = cache_v_win[0].reshape(nseq, WINDOW, KV_WIDTH)
    z_s, nk_s, nv_s, nc_s = _sample_mixer(p_s, ck, cv, state_conv[0], stabs, sink_col, wdw, bdw, lng, lnb)
    y_sample = _out_project(z_s, xs, w_out_b, gf).reshape(nseq, dec, D_MODEL)

    kv_shape = (1, -1, WINDOW, N_KV_HEADS, HEAD_DIM)
    return (y_prompt, y_sample, nk_p.reshape(kv_shape), nv_p.reshape(kv_shape), nc_p[None],
            nk_s.reshape(kv_shape), nv_s.reshape(kv_shape), nc_s[None])
```

```python
import functools

import jax
import jax.numpy as jnp
from jax import lax
from jax.experimental import pallas as pl
from jax.experimental.pallas import tpu as pltpu

D_MODEL = 2048
N_META = 16
D_ATTN = 1024
D_CONV = 1024
HEAD_DIM = 64
N_HEADS = 16
N_KV_HEADS = 2
KV_WIDTH = N_KV_HEADS * HEAD_DIM
WINDOW = 128
BLOCK = 128
ROPE_DIM = 16
ROPE_THETA = 500000.0
CONV_WIDTH = 31
PREFIX = CONV_WIDTH - 1
D_PROJ = 2 * D_ATTN + 2 * KV_WIDTH + 3 * D_CONV
EPS = 1e-5
PAST_LEN = 8192

Q0 = 0
K0 = D_ATTN
V0 = K0 + KV_WIDTH
GA0 = V0 + KV_WIDTH
A0 = GA0 + D_ATTN
B0 = A0 + D_CONV
GC0 = B0 + D_CONV

LANES = 128
SUBLANES = 8
CHUNKS = D_ATTN // LANES
CHUNKS_PER_KV = CHUNKS // N_KV_HEADS
PAD = 32
NEG = -1e30

TQ = 256
VMEM_LIMIT = 56 * 1024 * 1024

f32 = jnp.float32
bf16 = jnp.bfloat16


def _rms_norm(x, g):
    return x * lax.rsqrt(jnp.mean(x * x, axis=-1, keepdims=True) + EPS) * g


def _sigmoid(x):
    return 1.0 / (1.0 + jnp.exp(-x))


def _silu(x):
    return x * _sigmoid(x)


def _rope(x, cos, sin_prev, sin_next):
    outs = []
    for c in range(x.shape[1] // LANES):
        xc = x[:, c * LANES:(c + 1) * LANES]
        outs.append(xc * cos + pltpu.roll(xc, 8, 1) * sin_prev + pltpu.roll(xc, LANES - 8, 1) * sin_next)
    return outs[0] if len(outs) == 1 else jnp.concatenate(outs, axis=1)


def _half_variants(x):
    lo = lax.broadcasted_iota(jnp.int32, x.shape, 1) < HEAD_DIM
    xr = pltpu.roll(x, HEAD_DIM, 1)
    zero = jnp.zeros_like(x)
    return (jnp.where(lo, x, zero).astype(bf16), jnp.where(lo, zero, xr).astype(bf16),
            jnp.where(lo, xr, zero).astype(bf16), jnp.where(lo, zero, x).astype(bf16))


def _value_variants_t(v):
    vt = v.T
    top, bot = vt[0:HEAD_DIM], vt[HEAD_DIM:2 * HEAD_DIM]
    zero = jnp.zeros_like(top)
    return tuple(jnp.concatenate(pair, axis=0).astype(bf16)
                 for pair in ((top, zero), (zero, top), (bot, zero), (zero, bot)))


def _layer_norm(x, g, b):
    mu = jnp.mean(x, axis=-1, keepdims=True)
    xc = x - mu
    return xc * lax.rsqrt(jnp.mean(xc * xc, axis=-1, keepdims=True) + EPS) * g + b


def _rope_tables(pos):
    half = ROPE_DIM // 2
    inv = ROPE_THETA ** (-jnp.arange(0, ROPE_DIM, 2, dtype=f32) / ROPE_DIM)
    ang = pos.astype(f32)[:, None] * inv[None, :]
    cos, sin = jnp.cos(ang), jnp.sin(ang)
    t = pos.shape[0]
    ones = jnp.ones((t, HEAD_DIM - ROPE_DIM), f32)
    zeros = jnp.zeros((t, HEAD_DIM - ROPE_DIM), f32)
    zh = jnp.zeros((t, half), f32)
    cos64 = jnp.concatenate([cos, cos, ones], axis=1)
    prev64 = jnp.concatenate([zh, sin, zeros], axis=1)
    next64 = jnp.concatenate([-sin, zh, zeros], axis=1)
    rep = LANES // HEAD_DIM
    return jnp.tile(cos64, (1, rep)), jnp.tile(prev64, (1, rep)), jnp.tile(next64, (1, rep))


PROJ_BN = 768


def _proj_kernel(x_ref, meta_ref, g_ref, w_ref, p_ref, pm_ref, u_ref):
    n = x_ref.shape[0]

    @pl.when(pl.program_id(0) == 0)
    def _():
        u_ref[0:n, :] = _rms_norm(x_ref[...], g_ref[...]).astype(bf16)
        u_ref[n:n + N_META, :] = _rms_norm(meta_ref[...], g_ref[...]).astype(bf16)

    p = jnp.dot(u_ref[...], w_ref[...], preferred_element_type=f32)
    p_ref[...] = p[0:n]
    pm_ref[...] = p[n:n + N_META]


def _project_sample(x, meta, g, w_in):
    n = x.shape[0]
    once = pl.Buffered(1)
    return pl.pallas_call(
        _proj_kernel,
        grid=(D_PROJ // PROJ_BN,),
        in_specs=[
            pl.BlockSpec((n, D_MODEL), lambda j: (0, 0), pipeline_mode=once),
            pl.BlockSpec((N_META, D_MODEL), lambda j: (0, 0), pipeline_mode=once),
            pl.BlockSpec((1, D_MODEL), lambda j: (0, 0), pipeline_mode=once),
            pl.BlockSpec((D_MODEL, PROJ_BN), lambda j: (0, j)),
        ],
        out_specs=[
            pl.BlockSpec((n, PROJ_BN), lambda j: (0, j)),
            pl.BlockSpec((N_META, PROJ_BN), lambda j: (0, j)),
        ],
        out_shape=[jax.ShapeDtypeStruct((n, D_PROJ), f32), jax.ShapeDtypeStruct((N_META, D_PROJ), f32)],
        scratch_shapes=[pltpu.VMEM((n + N_META, D_MODEL), bf16)],
        compiler_params=pltpu.CompilerParams(dimension_semantics=("arbitrary",), vmem_limit_bytes=VMEM_LIMIT),
    )(x, meta, g, w_in)


CONV_OFF = PAD - PREFIX
CONV_TAP_TILES = (CONV_OFF + CONV_WIDTH - 1) // SUBLANES + 1


def _conv_chunk(load_tile, n_seq, n_tiles, wdw_ref, c):
    row = lax.broadcasted_iota(jnp.int32, (SUBLANES, LANES), 0)
    lanes = slice(c * LANES, (c + 1) * LANES)
    w = [jnp.broadcast_to(wdw_ref[k:k + 1, lanes], (SUBLANES, LANES)) for k in range(CONV_WIDTH)]
    outs = []
    for s in range(n_seq):
        g = [load_tile(s, j, lanes) for j in range(n_tiles + CONV_TAP_TILES - 1)]

        def z(r, i):
            acc = None
            for a in range(CONV_TAP_TILES):
                k = SUBLANES * a + r - CONV_OFF
                if 0 <= k < CONV_WIDTH:
                    term = w[k] * g[i + a]
                    acc = term if acc is None else acc + term
            return acc

        prev = [z(r, 0) for r in range(1, SUBLANES)]
        for i in range(n_tiles):
            nxt = [z(r, i + 1) for r in range(1, SUBLANES)]
            y = z(0, i)
            for r in range(1, SUBLANES):
                y = y + pltpu.roll(jnp.where(row >= r, prev[r - 1], nxt[r - 1]), SUBLANES - r, 0)
            outs.append(y)
            prev = nxt
    return jnp.concatenate(outs, axis=0)


def _causal_conv(load_tile, n_seq, n_tiles, wdw_ref):
    return jnp.concatenate([_conv_chunk(load_tile, n_seq, n_tiles, wdw_ref, c)
                            for c in range(D_CONV // LANES)], axis=1)


def _conv_ln(y, bdw_ref, lng_ref, lnb_ref):
    return _silu(_layer_norm(y + bdw_ref[...], lng_ref[...], lnb_ref[...]))


def _band_bias_t(first_valid_key):
    key = lax.broadcasted_iota(jnp.int32, (2 * BLOCK, BLOCK), 0)
    qry = lax.broadcasted_iota(jnp.int32, (2 * BLOCK, BLOCK), 1)
    valid = (key >= qry) & (key <= qry + WINDOW) & (key >= first_valid_key)
    return jnp.where(valid, 0.0, NEG).astype(f32)


def _attn_piece(q_blk, kvar_ref, vt_ref, row0, bias_t, sinks_ref, kv, par):
    qs = jnp.concatenate([q_blk[:, i * LANES:(i + 1) * LANES] for i in range(CHUNKS_PER_KV)], axis=0)
    kk = kvar_ref[kv * 2 + par, row0:row0 + 2 * BLOCK, :]
    vt = vt_ref[kv * 2 + par, :, row0:row0 + 2 * BLOCK]
    st = lax.dot_general(kk, qs, (((1,), (1,)), ((), ())), preferred_element_type=f32)
    es, invs = [], []
    for i in range(CHUNKS_PER_KV):
        blk = st[:, i * BLOCK:(i + 1) * BLOCK] + bias_t
        sink = sinks_ref[kv * 8 + 2 * i + par]
        m = jnp.maximum(jnp.max(blk, axis=0, keepdims=True), sink)
        e = jnp.exp(blk - m)
        invs.append(1.0 / (jnp.sum(e, axis=0, keepdims=True) + jnp.exp(sink - m)))
        es.append(e.astype(bf16))
    out_t = jnp.dot(vt, jnp.concatenate(es, axis=1), preferred_element_type=f32)
    return out_t * jnp.concatenate(invs, axis=1)


PROJ_PIECE = 512
OUT_PIECE = 512
CONV_PIECE = 256


def _prompt_kernel(sinks_ref, x_ref, cos_ref, sp_ref, sn_ref, pm_ref, mcos_ref, msp_ref, msn_ref,
                   g_ref, win_ref, wdw_ref, bdw_ref, lng_ref, lnb_ref, wout_ref, gf_ref,
                   y_ref, nk_ref, nv_ref, nc_ref, kvar_ref, vt_ref, glu_ref, u_ref, z_ref):
    t = pl.program_id(1)
    nt = pl.num_programs(1)

    @pl.when(t == 0)
    def _():
        pm = pm_ref[...]
        km = _rope(pm[:, K0:K0 + KV_WIDTH], mcos_ref[...], msp_ref[...], msn_ref[...])
        vm = pm[:, V0:V0 + KV_WIDTH]
        invalid = jnp.zeros((BLOCK - N_META, LANES), f32)
        kvar_ref[:, 0:BLOCK - N_META, :] = jnp.zeros((4, BLOCK - N_META, LANES), bf16)
        for i, var in enumerate(_half_variants(km)):
            kvar_ref[i, BLOCK - N_META:BLOCK, :] = var
        for i, var in enumerate(_value_variants_t(jnp.concatenate([invalid, vm], axis=0))):
            vt_ref[i, :, 0:BLOCK] = var
        glu_ref[0:PAD - N_META, :] = jnp.zeros((PAD - N_META, D_CONV), f32)
        glu_ref[PAD - N_META:PAD, :] = pm[:, A0:A0 + D_CONV] * _sigmoid(pm[:, B0:B0 + D_CONV])

    u_ref[...] = _rms_norm(x_ref[0], g_ref[...]).astype(bf16)

    def proj(c0, width):
        return jnp.dot(u_ref[...], win_ref[:, c0:c0 + width], preferred_element_type=f32)

    cos, sp, sn = cos_ref[...], sp_ref[...], sn_ref[...]

    kv = proj(K0, 2 * KV_WIDTH)
    k = _rope(kv[:, 0:KV_WIDTH], cos, sp, sn)
    v = kv[:, KV_WIDTH:2 * KV_WIDTH]
    for i, var in enumerate(_half_variants(k)):
        kvar_ref[i, BLOCK:BLOCK + TQ, :] = var
    for j in range(TQ // BLOCK):
        for i, var in enumerate(_value_variants_t(v[j * BLOCK:(j + 1) * BLOCK])):
            vt_ref[i, :, (j + 1) * BLOCK:(j + 2) * BLOCK] = var

    def load_tile(s, i, lanes):
        return glu_ref[i * SUBLANES:(i + 1) * SUBLANES, lanes]

    conv = []
    for p in range(D_CONV // CONV_PIECE):
        c0 = p * CONV_PIECE
        glu_ref[PAD:PAD + TQ, c0:c0 + CONV_PIECE] = proj(A0 + c0, CONV_PIECE) * _sigmoid(proj(B0 + c0, CONV_PIECE))
        conv.append(jnp.concatenate(
            [_conv_chunk(load_tile, 1, TQ // SUBLANES, wdw_ref, c) for c in range(c0 // LANES, (c0 + CONV_PIECE) // LANES)],
            axis=1))
    gc = jnp.concatenate([proj(GC0 + c, PROJ_PIECE) for c in range(0, D_CONV, PROJ_PIECE)], axis=1)
    z_conv = _conv_ln(jnp.concatenate(conv, axis=1), bdw_ref, lng_ref, lnb_ref) * _silu(gc)
    z_ref[:, D_ATTN:D_ATTN + D_CONV] = z_conv.astype(bf16)

    group = CHUNKS_PER_KV * LANES
    for kvh in range(N_KV_HEADS):
        c0 = kvh * group
        qh = (_rope(proj(Q0 + c0, group), cos, sp, sn) * (HEAD_DIM ** -0.5)).astype(bf16)
        blocks = []
        for j in range(TQ // BLOCK):
            first_valid = jnp.where(t == 0, BLOCK - N_META, 0) if j == 0 else 0
            bias_t = _band_bias_t(first_valid)
            qb = qh[j * BLOCK:(j + 1) * BLOCK]
            acc = (_attn_piece(qb, kvar_ref, vt_ref, j * BLOCK, bias_t, sinks_ref, kvh, 0)
                   + _attn_piece(qb, kvar_ref, vt_ref, j * BLOCK, bias_t, sinks_ref, kvh, 1))
            blocks.append(jnp.concatenate([acc[:, i * BLOCK:(i + 1) * BLOCK].T for i in range(CHUNKS_PER_KV)], axis=1))
        z_attn = jnp.concatenate(blocks, axis=0) * _silu(proj(GA0 + c0, group))
        z_ref[:, c0:c0 + group] = z_attn.astype(bf16)

    hs, ssq = [], None
    for c in range(0, D_MODEL, OUT_PIECE):
        h = x_ref[0, :, c:c + OUT_PIECE] + jnp.dot(z_ref[...], wout_ref[:, c:c + OUT_PIECE],
                                                   preferred_element_type=f32)
        part = jnp.sum(h * h, axis=-1, keepdims=True)
        ssq = part if ssq is None else ssq + part
        hs.append(h)
    scale = lax.rsqrt(ssq * (1.0 / D_MODEL) + EPS)
    for i, h in enumerate(hs):
        y_ref[0, :, i * OUT_PIECE:(i + 1) * OUT_PIECE] = h * scale * gf_ref[:, i * OUT_PIECE:(i + 1) * OUT_PIECE]

    @pl.when(t == nt - 1)
    def _():
        nk_ref[0] = k[TQ - WINDOW:TQ]
        nv_ref[0] = v[TQ - WINDOW:TQ]
        nc_ref[0] = glu_ref[PAD + TQ - PREFIX:PAD + TQ, :]

    kvar_ref[:, 0:BLOCK, :] = kvar_ref[:, TQ:TQ + BLOCK, :]
    vt_ref[:, :, 0:BLOCK] = vt_ref[:, :, TQ:TQ + BLOCK]
    glu_ref[0:PAD, :] = glu_ref[TQ:TQ + PAD, :]


def _prompt_call(sinks, x, tabs, p_meta, mtabs, g, w_in, w_dw, b_dw, ln_g, ln_b, w_out, gf):
    nb, seq, _ = x.shape
    nt = seq // TQ
    once = pl.Buffered(1)

    def const(shape):
        return pl.BlockSpec(shape, lambda b, t: (0,) * len(shape), pipeline_mode=once)

    tab_spec = pl.BlockSpec((TQ, LANES), lambda b, t: (t, 0))
    return pl.pallas_call(
        _prompt_kernel,
        grid=(nb, nt),
        in_specs=[
            pl.BlockSpec(memory_space=pltpu.SMEM),
            pl.BlockSpec((1, TQ, D_MODEL), lambda b, t: (b, t, 0)),
            tab_spec, tab_spec, tab_spec,
            const((N_META, D_PROJ)), const((N_META, LANES)), const((N_META, LANES)), const((N_META, LANES)),
            const((1, D_MODEL)), const((D_MODEL, D_PROJ)),
            const((CONV_WIDTH, D_CONV)), const((1, D_CONV)), const((1, D_CONV)), const((1, D_CONV)),
            const((D_ATTN + D_CONV, D_MODEL)), const((1, D_MODEL)),
        ],
        out_specs=[
            pl.BlockSpec((1, TQ, D_MODEL), lambda b, t: (b, t, 0)),
            pl.BlockSpec((1, WINDOW, KV_WIDTH), lambda b, t: (b, 0, 0)),
            pl.BlockSpec((1, WINDOW, KV_WIDTH), lambda b, t: (b, 0, 0)),
            pl.BlockSpec((1, PREFIX, D_CONV), lambda b, t: (b, 0, 0)),
        ],
        out_shape=[
            jax.ShapeDtypeStruct((nb, seq, D_MODEL), f32),
            jax.ShapeDtypeStruct((nb, WINDOW, KV_WIDTH), f32),
            jax.ShapeDtypeStruct((nb, WINDOW, KV_WIDTH), f32),
            jax.ShapeDtypeStruct((nb, PREFIX, D_CONV), f32),
        ],
        scratch_shapes=[
            pltpu.VMEM((4, BLOCK + TQ, LANES), bf16),
            pltpu.VMEM((4, LANES, BLOCK + TQ), bf16),
            pltpu.VMEM((PAD + TQ, D_CONV), f32),
            pltpu.VMEM((TQ, D_MODEL), bf16),
            pltpu.VMEM((TQ, D_ATTN + D_CONV), bf16),
        ],
        compiler_params=pltpu.CompilerParams(
            dimension_semantics=("arbitrary", "arbitrary"), vmem_limit_bytes=VMEM_LIMIT),
    )(sinks, x, *tabs, p_meta, *mtabs, g, w_in, w_dw, b_dw, ln_g, ln_b, w_out, gf)


SEQ_BLOCK = 16
SEQ_UNROLL = 4
DEC = 8
KEYS = 2 * WINDOW


def _sample_mixer_kernel(p_ref, ck_ref, cv_ref, st_ref, cos_ref, sp_ref, sn_ref, sink_ref,
                         wdw_ref, bdw_ref, lng_ref, lnb_ref,
                         z_ref, nk_ref, nv_ref, nc_ref, q_scr, k_scr, v_scr, attn_scr, cbuf):
    rows = SEQ_BLOCK * DEC
    cos, sp, sn = cos_ref[...], sp_ref[...], sn_ref[...]

    k = _rope(p_ref[:, K0:K0 + KV_WIDTH], cos, sp, sn)
    v = p_ref[:, V0:V0 + KV_WIDTH]
    q_scr[...] = _rope(p_ref[:, Q0:Q0 + D_ATTN], cos, sp, sn) * (HEAD_DIM ** -0.5)
    k_scr[...] = k
    v_scr[...] = v

    nk_ref[:, 0:WINDOW - DEC, :] = ck_ref[:, DEC:WINDOW, :]
    nk_ref[:, WINDOW - DEC:WINDOW, :] = k.reshape(SEQ_BLOCK, DEC, KV_WIDTH)
    nv_ref[:, 0:WINDOW - DEC, :] = cv_ref[:, DEC:WINDOW, :]
    nv_ref[:, WINDOW - DEC:WINDOW, :] = v.reshape(SEQ_BLOCK, DEC, KV_WIDTH)

    glu = p_ref[:, A0:A0 + D_CONV] * _sigmoid(p_ref[:, B0:B0 + D_CONV])
    cbuf[:, 0:SUBLANES, :] = jnp.zeros((SEQ_BLOCK, SUBLANES, D_CONV), f32)
    cbuf[:, CONV_OFF:PAD, :] = st_ref[...]
    cbuf[:, PAD:PAD + DEC, :] = glu.reshape(SEQ_BLOCK, DEC, D_CONV)
    nc_ref[...] = cbuf[:, PAD + DEC - PREFIX:PAD + DEC, :]
    conv = _causal_conv(lambda s, j, lanes: cbuf[s, j * SUBLANES:(j + 1) * SUBLANES, lanes],
                        SEQ_BLOCK, DEC // SUBLANES, wdw_ref)
    z_conv = _conv_ln(conv, bdw_ref, lng_ref, lnb_ref) * _silu(p_ref[:, GC0:GC0 + D_CONV])

    row = lax.broadcasted_iota(jnp.int32, (N_HEADS * DEC, KEYS), 0) % DEC
    col = lax.broadcasted_iota(jnp.int32, (N_HEADS * DEC, KEYS), 1)
    valid = ((col < WINDOW) & (col >= row)) | ((col >= WINDOW) & (col - WINDOW <= row))
    bias = jnp.where(valid, 0.0, NEG).astype(f32)
    lo = lax.broadcasted_iota(jnp.int32, (DEC, LANES), 1) < HEAD_DIM
    zero = jnp.zeros((DEC, LANES), f32)
    pad_rows = jnp.zeros((KEYS - WINDOW - DEC, LANES), f32)
    sink = sink_ref[...]

    def body(s, carry):
        r0 = pl.multiple_of(s * DEC, DEC)
        qs = q_scr[pl.ds(r0, DEC), :]
        pieces = [None] * N_HEADS
        for c in range(CHUNKS):
            qc = qs[:, c * LANES:(c + 1) * LANES]
            qr = pltpu.roll(qc, HEAD_DIM, 1)
            if c < CHUNKS_PER_KV:
                pieces[2 * c], pieces[2 * c + 1] = jnp.where(lo, qc, zero), jnp.where(lo, qr, zero)
            else:
                pieces[2 * c], pieces[2 * c + 1] = jnp.where(lo, zero, qr), jnp.where(lo, zero, qc)
        qrows = jnp.concatenate(pieces, axis=0).astype(bf16)
        kf = jnp.concatenate([ck_ref[s], k_scr[pl.ds(r0, DEC), :], pad_rows], axis=0).astype(bf16)
        vf = jnp.concatenate([cv_ref[s], v_scr[pl.ds(r0, DEC), :], pad_rows], axis=0).astype(bf16)
        sc = lax.dot_general(qrows, kf, (((1,), (1,)), ((), ())), preferred_element_type=f32) + bias
        m = jnp.maximum(jnp.max(sc, axis=1, keepdims=True), sink)
        e = jnp.exp(sc - m)
        den = jnp.sum(e, axis=1, keepdims=True) + jnp.exp(sink - m)
        o = jnp.dot(e.astype(bf16), vf, preferred_element_type=f32) * (1.0 / den)
        outs = []
        for c in range(CHUNKS):
            ev, od = o[2 * c * DEC:(2 * c + 1) * DEC], o[(2 * c + 1) * DEC:(2 * c + 2) * DEC]
            if c < CHUNKS_PER_KV:
                outs.append(jnp.where(lo, ev, pltpu.roll(od, HEAD_DIM, 1)))
            else:
                outs.append(jnp.where(lo, pltpu.roll(ev, HEAD_DIM, 1), od))
        attn_scr[pl.ds(r0, DEC), :] = jnp.concatenate(outs, axis=1)
        return carry

    lax.fori_loop(0, SEQ_BLOCK, body, 0, unroll=SEQ_UNROLL)

    z_attn = attn_scr[...] * _silu(p_ref[:, GA0:GA0 + D_ATTN])
    z_ref[:, 0:D_ATTN] = z_attn.astype(bf16)
    z_ref[:, D_ATTN:D_ATTN + D_CONV] = z_conv.astype(bf16)


def _sample_mixer(p, ck, cv, st, tabs, sink_col, w_dw, b_dw, ln_g, ln_b):
    nseq = ck.shape[0]
    rows = SEQ_BLOCK * DEC

    def const(shape):
        return pl.BlockSpec(shape, lambda i: (0,) * len(shape))

    return pl.pallas_call(
        _sample_mixer_kernel,
        grid=(nseq // SEQ_BLOCK,),
        in_specs=[
            pl.BlockSpec((rows, D_PROJ), lambda i: (i, 0)),
            pl.BlockSpec((SEQ_BLOCK, WINDOW, KV_WIDTH), lambda i: (i, 0, 0)),
            pl.BlockSpec((SEQ_BLOCK, WINDOW, KV_WIDTH), lambda i: (i, 0, 0)),
            pl.BlockSpec((SEQ_BLOCK, PREFIX, D_CONV), lambda i: (i, 0, 0)),
            const((rows, LANES)), const((rows, LANES)), const((rows, LANES)),
            const((N_HEADS * DEC, 1)),
            const((CONV_WIDTH, D_CONV)), const((1, D_CONV)), const((1, D_CONV)), const((1, D_CONV)),
        ],
        out_specs=[
            pl.BlockSpec((rows, D_ATTN + D_CONV), lambda i: (i, 0)),
            pl.BlockSpec((SEQ_BLOCK, WINDOW, KV_WIDTH), lambda i: (i, 0, 0)),
            pl.BlockSpec((SEQ_BLOCK, WINDOW, KV_WIDTH), lambda i: (i, 0, 0)),
            pl.BlockSpec((SEQ_BLOCK, PREFIX, D_CONV), lambda i: (i, 0, 0)),
        ],
        out_shape=[
            jax.ShapeDtypeStruct((nseq * DEC, D_ATTN + D_CONV), bf16),
            jax.ShapeDtypeStruct((nseq, WINDOW, KV_WIDTH), f32),
            jax.ShapeDtypeStruct((nseq, WINDOW, KV_WIDTH), f32),
            jax.ShapeDtypeStruct((nseq, PREFIX, D_CONV), f32),
        ],
        scratch_shapes=[
            pltpu.VMEM((rows, D_ATTN), f32),
            pltpu.VMEM((rows, KV_WIDTH), f32),
            pltpu.VMEM((rows, KV_WIDTH), f32),
            pltpu.VMEM((rows, D_ATTN), f32),
            pltpu.VMEM((SEQ_BLOCK, PAD + DEC, D_CONV), f32),
        ],
        compiler_params=pltpu.CompilerParams(dimension_semantics=("arbitrary",), vmem_limit_bytes=VMEM_LIMIT),
    )(p, ck, cv, st, *tabs, sink_col, w_dw, b_dw, ln_g, ln_b)


OUT_BM = 512


def _out_kernel(z_ref, x_ref, w_ref, gf_ref, y_ref):
    h = x_ref[...] + jnp.dot(z_ref[...], w_ref[...], preferred_element_type=f32)
    y_ref[...] = _rms_norm(h, gf_ref[...])


def _out_project(z, x, w_out, gf):
    n = x.shape[0]
    return pl.pallas_call(
        _out_kernel,
        grid=(n // OUT_BM,),
        in_specs=[
            pl.BlockSpec((OUT_BM, D_ATTN + D_CONV), lambda i: (i, 0)),
            pl.BlockSpec((OUT_BM, D_MODEL), lambda i: (i, 0)),
            pl.BlockSpec((D_ATTN + D_CONV, D_MODEL), lambda i: (0, 0), pipeline_mode=pl.Buffered(1)),
            pl.BlockSpec((1, D_MODEL), lambda i: (0, 0)),
        ],
        out_specs=pl.BlockSpec((OUT_BM, D_MODEL), lambda i: (i, 0)),
        out_shape=jax.ShapeDtypeStruct((n, D_MODEL), f32),
        compiler_params=pltpu.CompilerParams(dimension_semantics=("arbitrary",), vmem_limit_bytes=VMEM_LIMIT),
    )(z, x, w_out, gf)


def kernel(x_prompt, x_sample, cache_k_win, cache_v_win, state_conv, meta_tokens, norm_gain, w_in,
           attn_sinks, w_dw, b_dw, conv_norm_gain, conv_norm_bias, w_out, final_norm_gain):
    nb, seq, _ = x_prompt.shape
    nseq, dec, _ = x_sample.shape
    assert norm_gain.shape[0] == 1 and dec == DEC and seq % TQ == 0 and nseq % SEQ_BLOCK == 0

    g = norm_gain[0][None]
    w_in_b = w_in[0].astype(bf16)
    w_out_b = w_out[0].astype(bf16)
    sinks = attn_sinks[0]
    wdw, bdw = w_dw[0], b_dw[0][None]
    lng, lnb = conv_norm_gain[0][None], conv_norm_bias[0][None]
    gf = final_norm_gain[None]

    xs = x_sample.reshape(nseq * dec, D_MODEL)
    p_s, p_meta = _project_sample(xs, meta_tokens, g, w_in_b)

    tabs = _rope_tables(N_META + jnp.arange(seq))
    mtabs = _rope_tables(jnp.arange(N_META))
    y_prompt, nk_p, nv_p, nc_p = _prompt_call(sinks, x_prompt, tabs, p_meta, mtabs, g, w_in_b,
                                               wdw, bdw, lng, lnb, w_out_b, gf)

    stabs = _rope_tables(PAST_LEN + jnp.arange(SEQ_BLOCK * dec) % dec)
    sink_col = jnp.repeat(sinks, dec)[:, None]
    ck = cache_k_win[0].reshape(nseq, WINDOW, KV_WIDTH)
    cv = cache_v_win[0].reshape(nseq, WINDOW, KV_WIDTH)
    z_s, nk_s, nv_s, nc_s = _sample_mixer(p_s, ck, cv, state_conv[0], stabs, sink_col, wdw, bdw, lng, lnb)
    y_sample = _out_project(z_s, xs, w_out_b, gf).reshape(nseq, dec, D_MODEL)

    kv_shape = (1, -1, WINDOW, N_KV_HEADS, HEAD_DIM)
    return (y_prompt, y_sample, nk_p.reshape(kv_shape), nv_p.reshape(kv_shape), nc_p[None],
            nk_s.reshape(kv_shape), nv_s.reshape(kv_shape), nc_s[None])
```

```python
import functools

import jax
import jax.numpy as jnp
from jax import lax
from jax.experimental import pallas as pl
from jax.experimental.pallas import tpu as pltpu

D_MODEL = 2048
N_META = 16
D_ATTN = 1024
D_CONV = 1024
HEAD_DIM = 64
N_HEADS = 16
N_KV_HEADS = 2
KV_WIDTH = N_KV_HEADS * HEAD_DIM
WINDOW = 128
BLOCK = 128
ROPE_DIM = 16
ROPE_THETA = 500000.0
CONV_WIDTH = 31
PREFIX = CONV_WIDTH - 1
D_PROJ = 2 * D_ATTN + 2 * KV_WIDTH + 3 * D_CONV
EPS = 1e-5
PAST_LEN = 8192

Q0 = 0
K0 = D_ATTN
V0 = K0 + KV_WIDTH
GA0 = V0 + KV_WIDTH
A0 = GA0 + D_ATTN
B0 = A0 + D_CONV
GC0 = B0 + D_CONV

LANES = 128
SUBLANES = 8
CHUNKS = D_ATTN // LANES
CHUNKS_PER_KV = CHUNKS // N_KV_HEADS
PAD = 32
NEG = -1e30

TQ = 256
VMEM_LIMIT = 56 * 1024 * 1024

f32 = jnp.float32
bf16 = jnp.bfloat16


def _rms_norm(x, g):
    return x * lax.rsqrt(jnp.mean(x * x, axis=-1, keepdims=True) + EPS) * g


def _sigmoid(x):
    return 1.0 / (1.0 + jnp.exp(-x))


def _silu(x):
    return x * _sigmoid(x)


def _rope(x, cos, sin_prev, sin_next):
    outs = []
    for c in range(x.shape[1] // LANES):
        xc = x[:, c * LANES:(c + 1) * LANES]
        outs.append(xc * cos + pltpu.roll(xc, 8, 1) * sin_prev + pltpu.roll(xc, LANES - 8, 1) * sin_next)
    return outs[0] if len(outs) == 1 else jnp.concatenate(outs, axis=1)


def _half_variants(x):
    lo = lax.broadcasted_iota(jnp.int32, x.shape, 1) < HEAD_DIM
    xr = pltpu.roll(x, HEAD_DIM, 1)
    zero = jnp.zeros_like(x)
    return (jnp.where(lo, x, zero).astype(bf16), jnp.where(lo, zero, xr).astype(bf16),
            jnp.where(lo, xr, zero).astype(bf16), jnp.where(lo, zero, x).astype(bf16))


def _value_variants_t(v):
    vt = v.T
    top, bot = vt[0:HEAD_DIM], vt[HEAD_DIM:2 * HEAD_DIM]
    zero = jnp.zeros_like(top)
    return tuple(jnp.concatenate(pair, axis=0).astype(bf16)
                 for pair in ((top, zero), (zero, top), (bot, zero), (zero, bot)))


def _layer_norm(x, g, b):
    mu = jnp.mean(x, axis=-1, keepdims=True)
    xc = x - mu
    return xc * lax.rsqrt(jnp.mean(xc * xc, axis=-1, keepdims=True) + EPS) * g + b


def _rope_tables(pos):
    half = ROPE_DIM // 2
    inv = ROPE_THETA ** (-jnp.arange(0, ROPE_DIM, 2, dtype=f32) / ROPE_DIM)
    ang = pos.astype(f32)[:, None] * inv[None, :]
    cos, sin = jnp.cos(ang), jnp.sin(ang)
    t = pos.shape[0]
    ones = jnp.ones((t, HEAD_DIM - ROPE_DIM), f32)
    zeros = jnp.zeros((t, HEAD_DIM - ROPE_DIM), f32)
    zh = jnp.zeros((t, half), f32)
    cos64 = jnp.concatenate([cos, cos, ones], axis=1)
    prev64 = jnp.concatenate([zh, sin, zeros], axis=1)
    next64 = jnp.concatenate([-sin, zh, zeros], axis=1)
    rep = LANES // HEAD_DIM
    return jnp.tile(cos64, (1, rep)), jnp.tile(prev64, (1, rep)), jnp.tile(next64, (1, rep))


PROJ_BN = 768


def _proj_kernel(x_ref, meta_ref, g_ref, w_ref, p_ref, pm_ref, wb_ref, u_ref):
    n = x_ref.shape[0]

    @pl.when(pl.program_id(0) == 0)
    def _():
        u_ref[0:n, :] = _rms_norm(x_ref[...], g_ref[...]).astype(bf16)
        u_ref[n:n + N_META, :] = _rms_norm(meta_ref[...], g_ref[...]).astype(bf16)

    w = w_ref[...].astype(bf16)
    wb_ref[...] = w
    p = jnp.dot(u_ref[...], w, preferred_element_type=f32)
    p_ref[...] = p[0:n]
    pm_ref[...] = p[n:n + N_META]


def _project_sample(x, meta, g, w_in):
    n = x.shape[0]
    once = pl.Buffered(1)
    return pl.pallas_call(
        _proj_kernel,
        grid=(D_PROJ // PROJ_BN,),
        in_specs=[
            pl.BlockSpec((n, D_MODEL), lambda j: (0, 0), pipeline_mode=once),
            pl.BlockSpec((N_META, D_MODEL), lambda j: (0, 0), pipeline_mode=once),
            pl.BlockSpec((1, D_MODEL), lambda j: (0, 0), pipeline_mode=once),
            pl.BlockSpec((D_MODEL, PROJ_BN), lambda j: (0, j)),
        ],
        out_specs=[
            pl.BlockSpec((n, PROJ_BN), lambda j: (0, j)),
            pl.BlockSpec((N_META, PROJ_BN), lambda j: (0, j)),
            pl.BlockSpec((D_MODEL, PROJ_BN), lambda j: (0, j)),
        ],
        out_shape=[jax.ShapeDtypeStruct((n, D_PROJ), f32), jax.ShapeDtypeStruct((N_META, D_PROJ), f32),
                   jax.ShapeDtypeStruct((D_MODEL, D_PROJ), bf16)],
        scratch_shapes=[pltpu.VMEM((n + N_META, D_MODEL), bf16)],
        compiler_params=pltpu.CompilerParams(dimension_semantics=("arbitrary",), vmem_limit_bytes=VMEM_LIMIT),
    )(x, meta, g, w_in)


CONV_OFF = PAD - PREFIX
CONV_TAP_TILES = (CONV_OFF + CONV_WIDTH - 1) // SUBLANES + 1


def _conv_chunk(load_tile, n_seq, n_tiles, wdw_ref, c):
    row = lax.broadcasted_iota(jnp.int32, (SUBLANES, LANES), 0)
    lanes = slice(c * LANES, (c + 1) * LANES)
    w = [jnp.broadcast_to(wdw_ref[k:k + 1, lanes], (SUBLANES, LANES)) for k in range(CONV_WIDTH)]
    outs = []
    for s in range(n_seq):
        g = [load_tile(s, j, lanes) for j in range(n_tiles + CONV_TAP_TILES - 1)]

        def z(r, i):
            acc = None
            for a in range(CONV_TAP_TILES):
                k = SUBLANES * a + r - CONV_OFF
                if 0 <= k < CONV_WIDTH:
                    term = w[k] * g[i + a]
                    acc = term if acc is None else acc + term
            return acc

        prev = [z(r, 0) for r in range(1, SUBLANES)]
        for i in range(n_tiles):
            nxt = [z(r, i + 1) for r in range(1, SUBLANES)]
            y = z(0, i)
            for r in range(1, SUBLANES):
                y = y + pltpu.roll(jnp.where(row >= r, prev[r - 1], nxt[r - 1]), SUBLANES - r, 0)
            outs.append(y)
            prev = nxt
    return jnp.concatenate(outs, axis=0)


def _causal_conv(load_tile, n_seq, n_tiles, wdw_ref):
    return jnp.concatenate([_conv_chunk(load_tile, n_seq, n_tiles, wdw_ref, c)
                            for c in range(D_CONV // LANES)], axis=1)


def _conv_ln(y, bdw_ref, lng_ref, lnb_ref):
    return _silu(_layer_norm(y + bdw_ref[...], lng_ref[...], lnb_ref[...]))


def _band_bias_t(first_valid_key):
    key = lax.broadcasted_iota(jnp.int32, (2 * BLOCK, BLOCK), 0)
    qry = lax.broadcasted_iota(jnp.int32, (2 * BLOCK, BLOCK), 1)
    valid = (key >= qry) & (key <= qry + WINDOW) & (key >= first_valid_key)
    return jnp.where(valid, 0.0, NEG).astype(f32)


def _attn_piece(q_blk, kvar_ref, vt_ref, row0, bias_t, sinks_ref, kv, par):
    qs = jnp.concatenate([q_blk[:, i * LANES:(i + 1) * LANES] for i in range(CHUNKS_PER_KV)], axis=0)
    kk = kvar_ref[kv * 2 + par, row0:row0 + 2 * BLOCK, :]
    vt = vt_ref[kv * 2 + par, :, row0:row0 + 2 * BLOCK]
    st = lax.dot_general(kk, qs, (((1,), (1,)), ((), ())), preferred_element_type=f32)
    es, invs = [], []
    for i in range(CHUNKS_PER_KV):
        blk = st[:, i * BLOCK:(i + 1) * BLOCK] + bias_t
        sink = sinks_ref[kv * 8 + 2 * i + par]
        m = jnp.maximum(jnp.max(blk, axis=0, keepdims=True), sink)
        e = jnp.exp(blk - m)
        invs.append(1.0 / (jnp.sum(e, axis=0, keepdims=True) + jnp.exp(sink - m)))
        es.append(e.astype(bf16))
    out_t = jnp.dot(vt, jnp.concatenate(es, axis=1), preferred_element_type=f32)
    return out_t * jnp.concatenate(invs, axis=1)


PROJ_PIECE = 512
OUT_PIECE = 512
CONV_PIECE = 256


def _prompt_kernel(sinks_ref, x_ref, cos_ref, sp_ref, sn_ref, pm_ref, mcos_ref, msp_ref, msn_ref,
                   g_ref, win_ref, wdw_ref, bdw_ref, lng_ref, lnb_ref, wout_ref, gf_ref,
                   y_ref, nk_ref, nv_ref, nc_ref, kvar_ref, vt_ref, glu_ref, u_ref, z_ref):
    t = pl.program_id(1)
    nt = pl.num_programs(1)

    @pl.when(t == 0)
    def _():
        pm = pm_ref[...]
        km = _rope(pm[:, K0:K0 + KV_WIDTH], mcos_ref[...], msp_ref[...], msn_ref[...])
        vm = pm[:, V0:V0 + KV_WIDTH]
        invalid = jnp.zeros((BLOCK - N_META, LANES), f32)
        kvar_ref[:, 0:BLOCK - N_META, :] = jnp.zeros((4, BLOCK - N_META, LANES), bf16)
        for i, var in enumerate(_half_variants(km)):
            kvar_ref[i, BLOCK - N_META:BLOCK, :] = var
        for i, var in enumerate(_value_variants_t(jnp.concatenate([invalid, vm], axis=0))):
            vt_ref[i, :, 0:BLOCK] = var
        glu_ref[0:PAD - N_META, :] = jnp.zeros((PAD - N_META, D_CONV), f32)
        glu_ref[PAD - N_META:PAD, :] = pm[:, A0:A0 + D_CONV] * _sigmoid(pm[:, B0:B0 + D_CONV])

    u_ref[...] = _rms_norm(x_ref[0], g_ref[...]).astype(bf16)

    def proj(c0, width):
        return jnp.dot(u_ref[...], win_ref[:, c0:c0 + width], preferred_element_type=f32)

    cos, sp, sn = cos_ref[...], sp_ref[...], sn_ref[...]

    kv = proj(K0, 2 * KV_WIDTH)
    k = _rope(kv[:, 0:KV_WIDTH], cos, sp, sn)
    v = kv[:, KV_WIDTH:2 * KV_WIDTH]
    for i, var in enumerate(_half_variants(k)):
        kvar_ref[i, BLOCK:BLOCK + TQ, :] = var
    for j in range(TQ // BLOCK):
        for i, var in enumerate(_value_variants_t(v[j * BLOCK:(j + 1) * BLOCK])):
            vt_ref[i, :, (j + 1) * BLOCK:(j + 2) * BLOCK] = var

    def load_tile(s, i, lanes):
        return glu_ref[i * SUBLANES:(i + 1) * SUBLANES, lanes]

    def glu(c0, a, b):
        glu_ref[PAD:PAD + TQ, c0:c0 + PROJ_PIECE] = a * _sigmoid(b)

    def conv_piece(p):
        first = p * CONV_PIECE // LANES
        return jnp.concatenate([_conv_chunk(load_tile, 1, TQ // SUBLANES, wdw_ref, c)
                                for c in range(first, first + CONV_PIECE // LANES)], axis=1)

    def out_part(k0, width):
        return [jnp.dot(z_ref[:, k0:k0 + width], wout_ref[k0:k0 + width, c:c + OUT_PIECE],
                        preferred_element_type=f32) for c in range(0, D_MODEL, OUT_PIECE)]

    def attention(kvh, q_raw, ga_raw):
        qh = (_rope(q_raw, cos, sp, sn) * (HEAD_DIM ** -0.5)).astype(bf16)
        blocks = []
        for j in range(TQ // BLOCK):
            first_valid = jnp.where(t == 0, BLOCK - N_META, 0) if j == 0 else 0
            bias_t = _band_bias_t(first_valid)
            qb = qh[j * BLOCK:(j + 1) * BLOCK]
            acc = (_attn_piece(qb, kvar_ref, vt_ref, j * BLOCK, bias_t, sinks_ref, kvh, 0)
                   + _attn_piece(qb, kvar_ref, vt_ref, j * BLOCK, bias_t, sinks_ref, kvh, 1))
            blocks.append(jnp.concatenate([acc[:, i * BLOCK:(i + 1) * BLOCK].T for i in range(CHUNKS_PER_KV)], axis=1))
        z_attn = jnp.concatenate(blocks, axis=0) * _silu(ga_raw)
        z_ref[:, kvh * group:(kvh + 1) * group] = z_attn.astype(bf16)

    group = CHUNKS_PER_KV * LANES
    assert D_CONV == 2 * PROJ_PIECE and group == PROJ_PIECE and PROJ_PIECE == 2 * CONV_PIECE

    glu(0, proj(A0, PROJ_PIECE), proj(B0, PROJ_PIECE))
    a_hi = proj(A0 + PROJ_PIECE, PROJ_PIECE)
    conv = [conv_piece(0)]
    b_hi = proj(B0 + PROJ_PIECE, PROJ_PIECE)
    conv.append(conv_piece(1))
    glu(PROJ_PIECE, a_hi, b_hi)
    gc_lo = proj(GC0, PROJ_PIECE)
    conv.append(conv_piece(2))
    gc_hi = proj(GC0 + PROJ_PIECE, PROJ_PIECE)
    conv.append(conv_piece(3))
    q_lo = proj(Q0, group)
    z_conv = (_conv_ln(jnp.concatenate(conv, axis=1), bdw_ref, lng_ref, lnb_ref)
              * _silu(jnp.concatenate([gc_lo, gc_hi], axis=1)))
    z_ref[:, D_ATTN:D_ATTN + D_CONV] = z_conv.astype(bf16)

    q_hi = proj(Q0 + group, group)
    ga_lo = proj(GA0, group)
    attention(0, q_lo, ga_lo)
    ga_hi = proj(GA0 + group, group)
    o_conv = out_part(D_ATTN, D_CONV)
    attention(1, q_hi, ga_hi)
    o_lo = out_part(0, group)
    o_hi = out_part(group, group)

    hs, ssq = [], None
    for i, c in enumerate(range(0, D_MODEL, OUT_PIECE)):
        h = x_ref[0, :, c:c + OUT_PIECE] + (o_conv[i] + o_lo[i] + o_hi[i])
        part = jnp.sum(h * h, axis=-1, keepdims=True)
        ssq = part if ssq is None else ssq + part
        hs.append(h)
    scale = lax.rsqrt(ssq * (1.0 / D_MODEL) + EPS)
    for i, h in enumerate(hs):
        y_ref[0, :, i * OUT_PIECE:(i + 1) * OUT_PIECE] = h * scale * gf_ref[:, i * OUT_PIECE:(i + 1) * OUT_PIECE]

    @pl.when(t == nt - 1)
    def _():
        nk_ref[0] = k[TQ - WINDOW:TQ]
        nv_ref[0] = v[TQ - WINDOW:TQ]
        nc_ref[0] = glu_ref[PAD + TQ - PREFIX:PAD + TQ, :]

    kvar_ref[:, 0:BLOCK, :] = kvar_ref[:, TQ:TQ + BLOCK, :]
    vt_ref[:, :, 0:BLOCK] = vt_ref[:, :, TQ:TQ + BLOCK]
    glu_ref[0:PAD, :] = glu_ref[TQ:TQ + PAD, :]


def _prompt_call(sinks, x, tabs, p_meta, mtabs, g, w_in, w_dw, b_dw, ln_g, ln_b, w_out, gf):
    nb, seq, _ = x.shape
    nt = seq // TQ
    once = pl.Buffered(1)

    def const(shape):
        return pl.BlockSpec(shape, lambda b, t: (0,) * len(shape), pipeline_mode=once)

    tab_spec = pl.BlockSpec((TQ, LANES), lambda b, t: (t, 0))
    return pl.pallas_call(
        _prompt_kernel,
        grid=(nb, nt),
        in_specs=[
            pl.BlockSpec(memory_space=pltpu.SMEM),
            pl.BlockSpec((1, TQ, D_MODEL), lambda b, t: (b, t, 0)),
            tab_spec, tab_spec, tab_spec,
            const((N_META, D_PROJ)), const((N_META, LANES)), const((N_META, LANES)), const((N_META, LANES)),
            const((1, D_MODEL)), const((D_MODEL, D_PROJ)),
            const((CONV_WIDTH, D_CONV)), const((1, D_CONV)), const((1, D_CONV)), const((1, D_CONV)),
            const((D_ATTN + D_CONV, D_MODEL)), const((1, D_MODEL)),
        ],
        out_specs=[
            pl.BlockSpec((1, TQ, D_MODEL), lambda b, t: (b, t, 0)),
            pl.BlockSpec((1, WINDOW, KV_WIDTH), lambda b, t: (b, 0, 0)),
            pl.BlockSpec((1, WINDOW, KV_WIDTH), lambda b, t: (b, 0, 0)),
            pl.BlockSpec((1, PREFIX, D_CONV), lambda b, t: (b, 0, 0)),
        ],
        out_shape=[
            jax.ShapeDtypeStruct((nb, seq, D_MODEL), f32),
            jax.ShapeDtypeStruct((nb, WINDOW, KV_WIDTH), f32),
            jax.ShapeDtypeStruct((nb, WINDOW, KV_WIDTH), f32),
            jax.ShapeDtypeStruct((nb, PREFIX, D_CONV), f32),
        ],
        scratch_shapes=[
            pltpu.VMEM((4, BLOCK + TQ, LANES), bf16),
            pltpu.VMEM((4, LANES, BLOCK + TQ), bf16),
            pltpu.VMEM((PAD + TQ, D_CONV), f32),
            pltpu.VMEM((TQ, D_MODEL), bf16),
            pltpu.VMEM((TQ, D_ATTN + D_CONV), bf16),
        ],
        compiler_params=pltpu.CompilerParams(
            dimension_semantics=("arbitrary", "arbitrary"), vmem_limit_bytes=VMEM_LIMIT),
    )(sinks, x, *tabs, p_meta, *mtabs, g, w_in, w_dw, b_dw, ln_g, ln_b, w_out, gf)


SEQ_BLOCK = 16
SEQ_UNROLL = 4
DEC = 8
KEYS = 2 * WINDOW


def _sample_mixer_kernel(p_ref, ck_ref, cv_ref, st_ref, cos_ref, sp_ref, sn_ref, sink_ref,
                         wdw_ref, bdw_ref, lng_ref, lnb_ref,
                         z_ref, nk_ref, nv_ref, nc_ref, q_scr, k_scr, v_scr, attn_scr, cbuf):
    rows = SEQ_BLOCK * DEC
    cos, sp, sn = cos_ref[...], sp_ref[...], sn_ref[...]

    k = _rope(p_ref[:, K0:K0 + KV_WIDTH], cos, sp, sn)
    v = p_ref[:, V0:V0 + KV_WIDTH]
    q_scr[...] = _rope(p_ref[:, Q0:Q0 + D_ATTN], cos, sp, sn) * (HEAD_DIM ** -0.5)
    k_scr[...] = k
    v_scr[...] = v

    nk_ref[:, 0:WINDOW - DEC, :] = ck_ref[:, DEC:WINDOW, :]
    nk_ref[:, WINDOW - DEC:WINDOW, :] = k.reshape(SEQ_BLOCK, DEC, KV_WIDTH)
    nv_ref[:, 0:WINDOW - DEC, :] = cv_ref[:, DEC:WINDOW, :]
    nv_ref[:, WINDOW - DEC:WINDOW, :] = v.reshape(SEQ_BLOCK, DEC, KV_WIDTH)

    glu = p_ref[:, A0:A0 + D_CONV] * _sigmoid(p_ref[:, B0:B0 + D_CONV])
    cbuf[:, 0:SUBLANES, :] = jnp.zeros((SEQ_BLOCK, SUBLANES, D_CONV), f32)
    cbuf[:, CONV_OFF:PAD, :] = st_ref[...]
    cbuf[:, PAD:PAD + DEC, :] = glu.reshape(SEQ_BLOCK, DEC, D_CONV)
    nc_ref[...] = cbuf[:, PAD + DEC - PREFIX:PAD + DEC, :]
    conv = _causal_conv(lambda s, j, lanes: cbuf[s, j * SUBLANES:(j + 1) * SUBLANES, lanes],
                        SEQ_BLOCK, DEC // SUBLANES, wdw_ref)
    z_conv = _conv_ln(conv, bdw_ref, lng_ref, lnb_ref) * _silu(p_ref[:, GC0:GC0 + D_CONV])

    row = lax.broadcasted_iota(jnp.int32, (N_HEADS * DEC, KEYS), 0) % DEC
    col = lax.broadcasted_iota(jnp.int32, (N_HEADS * DEC, KEYS), 1)
    valid = ((col < WINDOW) & (col >= row)) | ((col >= WINDOW) & (col - WINDOW <= row))
    bias = jnp.where(valid, 0.0, NEG).astype(f32)
    lo = lax.broadcasted_iota(jnp.int32, (DEC, LANES), 1) < HEAD_DIM
    zero = jnp.zeros((DEC, LANES), f32)
    pad_rows = jnp.zeros((KEYS - WINDOW - DEC, LANES), f32)
    sink = sink_ref[...]

    def body(s, carry):
        r0 = pl.multiple_of(s * DEC, DEC)
        qs = q_scr[pl.ds(r0, DEC), :]
        pieces = [None] * N_HEADS
        for c in range(CHUNKS):
            qc = qs[:, c * LANES:(c + 1) * LANES]
            qr = pltpu.roll(qc, HEAD_DIM, 1)
            if c < CHUNKS_PER_KV:
                pieces[2 * c], pieces[2 * c + 1] = jnp.where(lo, qc, zero), jnp.where(lo, qr, zero)
            else:
                pieces[2 * c], pieces[2 * c + 1] = jnp.where(lo, zero, qr), jnp.where(lo, zero, qc)
        qrows = jnp.concatenate(pieces, axis=0).astype(bf16)
        kf = jnp.concatenate([ck_ref[s], k_scr[pl.ds(r0, DEC), :], pad_rows], axis=0).astype(bf16)
        vf = jnp.concatenate([cv_ref[s], v_scr[pl.ds(r0, DEC), :], pad_rows], axis=0).astype(bf16)
        sc = lax.dot_general(qrows, kf, (((1,), (1,)), ((), ())), preferred_element_type=f32) + bias
        m = jnp.maximum(jnp.broadcast_to(jnp.max(sc, axis=1, keepdims=True), sink.shape), sink)
        e = jnp.exp(sc - jnp.concatenate([m] * (KEYS // LANES), axis=1))
        den = jnp.broadcast_to(jnp.sum(e, axis=1, keepdims=True), sink.shape) + jnp.exp(sink - m)
        o = jnp.dot(e.astype(bf16), vf, preferred_element_type=f32) * (1.0 / den)
        outs = []
        for c in range(CHUNKS):
            ev, od = o[2 * c * DEC:(2 * c + 1) * DEC], o[(2 * c + 1) * DEC:(2 * c + 2) * DEC]
            if c < CHUNKS_PER_KV:
                outs.append(jnp.where(lo, ev, pltpu.roll(od, HEAD_DIM, 1)))
            else:
                outs.append(jnp.where(lo, pltpu.roll(ev, HEAD_DIM, 1), od))
        attn_scr[pl.ds(r0, DEC), :] = jnp.concatenate(outs, axis=1)
        return carry

    lax.fori_loop(0, SEQ_BLOCK, body, 0, unroll=SEQ_UNROLL)

    z_attn = attn_scr[...] * _silu(p_ref[:, GA0:GA0 + D_ATTN])
    z_ref[:, 0:D_ATTN] = z_attn.astype(bf16)
    z_ref[:, D_ATTN:D_ATTN + D_CONV] = z_conv.astype(bf16)


def _sample_mixer(p, ck, cv, st, tabs, sink_col, w_dw, b_dw, ln_g, ln_b):
    nseq = ck.shape[0]
    rows = SEQ_BLOCK * DEC

    def const(shape):
        return pl.BlockSpec(shape, lambda i: (0,) * len(shape))

    return pl.pallas_call(
        _sample_mixer_kernel,
        grid=(nseq // SEQ_BLOCK,),
        in_specs=[
            pl.BlockSpec((rows, D_PROJ), lambda i: (i, 0)),
            pl.BlockSpec((SEQ_BLOCK, WINDOW, KV_WIDTH), lambda i: (i, 0, 0)),
            pl.BlockSpec((SEQ_BLOCK, WINDOW, KV_WIDTH), lambda i: (i, 0, 0)),
            pl.BlockSpec((SEQ_BLOCK, PREFIX, D_CONV), lambda i: (i, 0, 0)),
            const((rows, LANES)), const((rows, LANES)), const((rows, LANES)),
            const((N_HEADS * DEC, LANES)),
            const((CONV_WIDTH, D_CONV)), const((1, D_CONV)), const((1, D_CONV)), const((1, D_CONV)),
        ],
        out_specs=[
            pl.BlockSpec((rows, D_ATTN + D_CONV), lambda i: (i, 0)),
            pl.BlockSpec((SEQ_BLOCK, WINDOW, KV_WIDTH), lambda i: (i, 0, 0)),
            pl.BlockSpec((SEQ_BLOCK, WINDOW, KV_WIDTH), lambda i: (i, 0, 0)),
            pl.BlockSpec((SEQ_BLOCK, PREFIX, D_CONV), lambda i: (i, 0, 0)),
        ],
        out_shape=[
            jax.ShapeDtypeStruct((nseq * DEC, D_ATTN + D_CONV), bf16),
            jax.ShapeDtypeStruct((nseq, WINDOW, KV_WIDTH), f32),
            jax.ShapeDtypeStruct((nseq, WINDOW, KV_WIDTH), f32),
            jax.ShapeDtypeStruct((nseq, PREFIX, D_CONV), f32),
        ],
        scratch_shapes=[
            pltpu.VMEM((rows, D_ATTN), f32),
            pltpu.VMEM((rows, KV_WIDTH), f32),
            pltpu.VMEM((rows, KV_WIDTH), f32),
            pltpu.VMEM((rows, D_ATTN), f32),
            pltpu.VMEM((SEQ_BLOCK, PAD + DEC, D_CONV), f32),
        ],
        compiler_params=pltpu.CompilerParams(dimension_semantics=("arbitrary",), vmem_limit_bytes=VMEM_LIMIT),
    )(p, ck, cv, st, *tabs, sink_col, w_dw, b_dw, ln_g, ln_b)


OUT_BM = 512


def _out_kernel(z_ref, x_ref, w_ref, gf_ref, y_ref):
    h = x_ref[...] + jnp.dot(z_ref[...], w_ref[...], preferred_element_type=f32)
    y_ref[...] = _rms_norm(h, gf_ref[...])


def _out_project(z, x, w_out, gf):
    n = x.shape[0]
    return pl.pallas_call(
        _out_kernel,
        grid=(n // OUT_BM,),
        in_specs=[
            pl.BlockSpec((OUT_BM, D_ATTN + D_CONV), lambda i: (i, 0)),
            pl.BlockSpec((OUT_BM, D_MODEL), lambda i: (i, 0)),
            pl.BlockSpec((D_ATTN + D_CONV, D_MODEL), lambda i: (0, 0), pipeline_mode=pl.Buffered(1)),
            pl.BlockSpec((1, D_MODEL), lambda i: (0, 0)),
        ],
        out_specs=pl.BlockSpec((OUT_BM, D_MODEL), lambda i: (i, 0)),
        out_shape=jax.ShapeDtypeStruct((n, D_MODEL), f32),
        compiler_params=pltpu.CompilerParams(dimension_semantics=("arbitrary",), vmem_limit_bytes=VMEM_LIMIT),
    )(z, x, w_out, gf)


def kernel(x_prompt, x_sample, cache_k_win, cache_v_win, state_conv, meta_tokens, norm_gain, w_in,
           attn_sinks, w_dw, b_dw, conv_norm_gain, conv_norm_bias, w_out, final_norm_gain):
    nb, seq, _ = x_prompt.shape
    nseq, dec, _ = x_sample.shape
    assert norm_gain.shape[0] == 1 and dec == DEC and seq % TQ == 0 and nseq % SEQ_BLOCK == 0

    g = norm_gain[0][None]
    w_out_b = w_out[0].astype(bf16)
    sinks = attn_sinks[0]
    wdw, bdw = w_dw[0], b_dw[0][None]
    lng, lnb = conv_norm_gain[0][None], conv_norm_bias[0][None]
    gf = final_norm_gain[None]

    xs = x_sample.reshape(nseq * dec, D_MODEL)
    p_s, p_meta, w_in_b = _project_sample(xs, meta_tokens, g, w_in[0])

    tabs = _rope_tables(N_META + jnp.arange(seq))
    mtabs = _rope_tables(jnp.arange(N_META))
    y_prompt, nk_p, nv_p, nc_p = _prompt_call(sinks, x_prompt, tabs, p_meta, mtabs, g, w_in_b,
                                               wdw, bdw, lng, lnb, w_out_b, gf)

    stabs = _rope_tables(PAST_LEN + jnp.arange(SEQ_BLOCK * dec) % dec)
    sink_col = jnp.broadcast_to(jnp.repeat(sinks, dec)[:, None], (N_HEADS * dec, LANES))
    ck = cache_k_win[0].reshape(nseq, WINDOW, KV_WIDTH)
    cv = cache_v_win[0].reshape(nseq, WINDOW, KV_WIDTH)
    z_s, nk_s, nv_s, nc_s = _sample_mixer(p_s, ck, cv, state_conv[0], stabs, sink_col, wdw, bdw, lng, lnb)
    y_sample = _out_project(z_s, xs, w_out_b, gf).reshape(nseq, dec, D_MODEL)

    kv_shape = (1, -1, WINDOW, N_KV_HEADS, HEAD_DIM)
    return (y_prompt, y_sample, nk_p.reshape(kv_shape), nv_p.reshape(kv_shape), nc_p[None],
            nk_s.reshape(kv_shape), nv_s.reshape(kv_shape), nc_s[None])
```

```python
import functools

import jax
import jax.numpy as jnp
from jax import lax
from jax.experimental import pallas as pl
from jax.experimental.pallas import tpu as pltpu

D_MODEL = 2048
N_META = 16
D_ATTN = 1024
D_CONV = 1024
HEAD_DIM = 64
N_HEADS = 16
N_KV_HEADS = 2
KV_WIDTH = N_KV_HEADS * HEAD_DIM
WINDOW = 128
BLOCK = 128
ROPE_DIM = 16
ROPE_THETA = 500000.0
CONV_WIDTH = 31
PREFIX = CONV_WIDTH - 1
D_PROJ = 2 * D_ATTN + 2 * KV_WIDTH + 3 * D_CONV
EPS = 1e-5
PAST_LEN = 8192

Q0 = 0
K0 = D_ATTN
V0 = K0 + KV_WIDTH
GA0 = V0 + KV_WIDTH
A0 = GA0 + D_ATTN
B0 = A0 + D_CONV
GC0 = B0 + D_CONV

LANES = 128
SUBLANES = 8
CHUNKS = D_ATTN // LANES
CHUNKS_PER_KV = CHUNKS // N_KV_HEADS
PAD = 32
NEG = -1e30

TQ = 256
VMEM_LIMIT = 56 * 1024 * 1024

f32 = jnp.float32
bf16 = jnp.bfloat16


def _rms_norm(x, g):
    return x * lax.rsqrt(jnp.mean(x * x, axis=-1, keepdims=True) + EPS) * g


NEG_LOG2_E = -1.4426950408889634


def _sigmoid(x):
    return 1.0 / (1.0 + jnp.exp2(x * NEG_LOG2_E))


def _silu(x):
    return x * _sigmoid(x)


def _rope(x, cos, sin_prev, sin_next):
    outs = []
    for c in range(x.shape[1] // LANES):
        xc = x[:, c * LANES:(c + 1) * LANES]
        outs.append(xc * cos + pltpu.roll(xc, 8, 1) * sin_prev + pltpu.roll(xc, LANES - 8, 1) * sin_next)
    return outs[0] if len(outs) == 1 else jnp.concatenate(outs, axis=1)


def _half_variants(x):
    lo = lax.broadcasted_iota(jnp.int32, x.shape, 1) < HEAD_DIM
    xr = pltpu.roll(x, HEAD_DIM, 1)
    zero = jnp.zeros_like(x)
    return (jnp.where(lo, x, zero).astype(bf16), jnp.where(lo, zero, xr).astype(bf16),
            jnp.where(lo, xr, zero).astype(bf16), jnp.where(lo, zero, x).astype(bf16))


def _value_variants_t(v):
    vt = v.T
    top, bot = vt[0:HEAD_DIM], vt[HEAD_DIM:2 * HEAD_DIM]
    ones = jnp.ones_like(top)
    return tuple(jnp.concatenate(pair, axis=0).astype(bf16)
                 for pair in ((top, ones), (ones, top), (bot, ones), (ones, bot)))


def _layer_norm(x, g, b):
    mu = jnp.mean(x, axis=-1, keepdims=True)
    xc = x - mu
    return xc * lax.rsqrt(jnp.mean(xc * xc, axis=-1, keepdims=True) + EPS) * g + b


def _rope_tables(pos):
    half = ROPE_DIM // 2
    inv = ROPE_THETA ** (-jnp.arange(0, ROPE_DIM, 2, dtype=f32) / ROPE_DIM)
    ang = pos.astype(f32)[:, None] * inv[None, :]
    cos, sin = jnp.cos(ang), jnp.sin(ang)
    t = pos.shape[0]
    ones = jnp.ones((t, HEAD_DIM - ROPE_DIM), f32)
    zeros = jnp.zeros((t, HEAD_DIM - ROPE_DIM), f32)
    zh = jnp.zeros((t, half), f32)
    cos64 = jnp.concatenate([cos, cos, ones], axis=1)
    prev64 = jnp.concatenate([zh, sin, zeros], axis=1)
    next64 = jnp.concatenate([-sin, zh, zeros], axis=1)
    rep = LANES // HEAD_DIM
    return jnp.tile(cos64, (1, rep)), jnp.tile(prev64, (1, rep)), jnp.tile(next64, (1, rep))


PROJ_BN = 768


def _proj_kernel(x_ref, meta_ref, g_ref, w_ref, p_ref, pm_ref, wb_ref, u_ref):
    n = x_ref.shape[0]

    @pl.when(pl.program_id(0) == 0)
    def _():
        u_ref[0:n, :] = _rms_norm(x_ref[...], g_ref[...]).astype(bf16)
        u_ref[n:n + N_META, :] = _rms_norm(meta_ref[...], g_ref[...]).astype(bf16)

    w = w_ref[...].astype(bf16)
    wb_ref[...] = w
    p = jnp.dot(u_ref[...], w, preferred_element_type=f32)
    p_ref[...] = p[0:n]
    pm_ref[...] = p[n:n + N_META]


def _project_sample(x, meta, g, w_in):
    n = x.shape[0]
    once = pl.Buffered(1)
    return pl.pallas_call(
        _proj_kernel,
        grid=(D_PROJ // PROJ_BN,),
        in_specs=[
            pl.BlockSpec((n, D_MODEL), lambda j: (0, 0), pipeline_mode=once),
            pl.BlockSpec((N_META, D_MODEL), lambda j: (0, 0), pipeline_mode=once),
            pl.BlockSpec((1, D_MODEL), lambda j: (0, 0), pipeline_mode=once),
            pl.BlockSpec((D_MODEL, PROJ_BN), lambda j: (0, j)),
        ],
        out_specs=[
            pl.BlockSpec((n, PROJ_BN), lambda j: (0, j)),
            pl.BlockSpec((N_META, PROJ_BN), lambda j: (0, j)),
            pl.BlockSpec((D_MODEL, PROJ_BN), lambda j: (0, j)),
        ],
        out_shape=[jax.ShapeDtypeStruct((n, D_PROJ), f32), jax.ShapeDtypeStruct((N_META, D_PROJ), f32),
                   jax.ShapeDtypeStruct((D_MODEL, D_PROJ), bf16)],
        scratch_shapes=[pltpu.VMEM((n + N_META, D_MODEL), bf16)],
        compiler_params=pltpu.CompilerParams(dimension_semantics=("arbitrary",), vmem_limit_bytes=VMEM_LIMIT),
    )(x, meta, g, w_in)


CONV_OFF = PAD - PREFIX
CONV_TAP_TILES = (CONV_OFF + CONV_WIDTH - 1) // SUBLANES + 1


def _conv_chunk(load_tile, n_seq, n_tiles, wdw_ref, c):
    row = lax.broadcasted_iota(jnp.int32, (SUBLANES, LANES), 0)
    lanes = slice(c * LANES, (c + 1) * LANES)
    w = [jnp.broadcast_to(wdw_ref[k:k + 1, lanes], (SUBLANES, LANES)) for k in range(CONV_WIDTH)]
    outs = []
    for s in range(n_seq):
        g = [load_tile(s, j, lanes) for j in range(n_tiles + CONV_TAP_TILES - 1)]

        def z(r, i):
            acc = None
            for a in range(CONV_TAP_TILES):
                k = SUBLANES * a + r - CONV_OFF
                if 0 <= k < CONV_WIDTH:
                    term = w[k] * g[i + a]
                    acc = term if acc is None else acc + term
            return acc

        prev = [z(r, 0) for r in range(1, SUBLANES)]
        for i in range(n_tiles):
            nxt = [z(r, i + 1) for r in range(1, SUBLANES)]
            y = z(0, i)
            for r in range(1, SUBLANES):
                y = y + pltpu.roll(jnp.where(row >= r, prev[r - 1], nxt[r - 1]), SUBLANES - r, 0)
            outs.append(y)
            prev = nxt
    return jnp.concatenate(outs, axis=0)


def _causal_conv(load_tile, n_seq, n_tiles, wdw_ref):
    return jnp.concatenate([_conv_chunk(load_tile, n_seq, n_tiles, wdw_ref, c)
                            for c in range(D_CONV // LANES)], axis=1)


def _conv_ln(y, bdw_ref, lng_ref, lnb_ref):
    return _silu(_layer_norm(y + bdw_ref[...], lng_ref[...], lnb_ref[...]))


def _band_bias_t(first_valid_key):
    key = lax.broadcasted_iota(jnp.int32, (2 * BLOCK, BLOCK), 0)
    qry = lax.broadcasted_iota(jnp.int32, (2 * BLOCK, BLOCK), 1)
    valid = (key >= qry) & (key <= qry + WINDOW) & (key >= first_valid_key)
    return jnp.where(valid, 0.0, NEG).astype(f32)


def _attn_piece(q_blk, kvar_ref, vt_ref, row0, bias_t, sinks_ref, kv, par):
    qs = jnp.concatenate([q_blk[:, i * LANES:(i + 1) * LANES] for i in range(CHUNKS_PER_KV)], axis=0)
    kk = kvar_ref[kv * 2 + par, row0:row0 + 2 * BLOCK, :]
    vt = vt_ref[kv * 2 + par, :, row0:row0 + 2 * BLOCK]
    st = lax.dot_general(kk, qs, (((1,), (1,)), ((), ())), preferred_element_type=f32)
    es, sink_terms = [], []
    for i in range(CHUNKS_PER_KV):
        blk = st[:, i * BLOCK:(i + 1) * BLOCK] + bias_t
        sink = sinks_ref[kv * 8 + 2 * i + par]
        m = jnp.maximum(jnp.max(blk, axis=0, keepdims=True), sink)
        es.append(jnp.exp(blk - m).astype(bf16))
        sink_terms.append(jnp.exp(sink - m))
    out_t = jnp.dot(vt, jnp.concatenate(es, axis=1), preferred_element_type=f32)
    vals, sums = (0, HEAD_DIM) if par == 0 else (HEAD_DIM, 0)
    den = out_t[sums:sums + 1, :] + jnp.concatenate(sink_terms, axis=1)
    return out_t[vals:vals + HEAD_DIM, :] * (1.0 / den)


PROJ_PIECE = 512
OUT_PIECE = 512
CONV_PIECE = 256


def _prompt_kernel(sinks_ref, x_ref, tab_ref, pm_ref, mcos_ref, msp_ref, msn_ref,
                   g_ref, win_ref, wdw_ref, bdw_ref, lng_ref, lnb_ref, wout_ref, gf_ref,
                   y_ref, nk_ref, nv_ref, nc_ref, kvar_ref, vt_ref, glu_ref, u_ref, z_ref):
    t = pl.program_id(1)
    nt = pl.num_programs(1)

    @pl.when(t == 0)
    def _():
        pm = pm_ref[...]
        km = _rope(pm[:, K0:K0 + KV_WIDTH], mcos_ref[...], msp_ref[...], msn_ref[...])
        vm = pm[:, V0:V0 + KV_WIDTH]
        invalid = jnp.zeros((BLOCK - N_META, LANES), f32)
        kvar_ref[:, 0:BLOCK - N_META, :] = jnp.zeros((4, BLOCK - N_META, LANES), bf16)
        for i, var in enumerate(_half_variants(km)):
            kvar_ref[i, BLOCK - N_META:BLOCK, :] = var
        for i, var in enumerate(_value_variants_t(jnp.concatenate([invalid, vm], axis=0))):
            vt_ref[i, :, 0:BLOCK] = var
        glu_ref[0:PAD - N_META, :] = jnp.zeros((PAD - N_META, D_CONV), f32)
        glu_ref[PAD - N_META:PAD, :] = pm[:, A0:A0 + D_CONV] * _sigmoid(pm[:, B0:B0 + D_CONV])

    u_ref[...] = _rms_norm(x_ref[0], g_ref[...]).astype(bf16)

    def proj(c0, width):
        return jnp.dot(u_ref[...], win_ref[:, c0:c0 + width], preferred_element_type=f32)

    cos, sp, sn, cos_q, sp_q, sn_q = (tab_ref[:, i * LANES:(i + 1) * LANES] for i in range(6))

    kv = proj(K0, 2 * KV_WIDTH)
    k = _rope(kv[:, 0:KV_WIDTH], cos, sp, sn)
    v = kv[:, KV_WIDTH:2 * KV_WIDTH]
    for i, var in enumerate(_half_variants(k)):
        kvar_ref[i, BLOCK:BLOCK + TQ, :] = var
    for j in range(TQ // BLOCK):
        for i, var in enumerate(_value_variants_t(v[j * BLOCK:(j + 1) * BLOCK])):
            vt_ref[i, :, (j + 1) * BLOCK:(j + 2) * BLOCK] = var

    def load_tile(s, i, lanes):
        return glu_ref[i * SUBLANES:(i + 1) * SUBLANES, lanes]

    def glu(c0, a, b):
        glu_ref[PAD:PAD + TQ, c0:c0 + PROJ_PIECE] = a * _sigmoid(b)

    def conv_piece(p):
        first = p * CONV_PIECE // LANES
        return jnp.concatenate([_conv_chunk(load_tile, 1, TQ // SUBLANES, wdw_ref, c)
                                for c in range(first, first + CONV_PIECE // LANES)], axis=1)

    def out_part(k0, width):
        lhs = z_ref[:, k0:k0 + width]
        return [jnp.dot(lhs, wout_ref[k0:k0 + width, c:c + OUT_PIECE], preferred_element_type=f32)
                for c in range(0, D_MODEL, OUT_PIECE)]

    def attention(kvh, q_raw, ga_raw):
        qh = _rope(q_raw, cos_q, sp_q, sn_q).astype(bf16)
        blocks = []
        for j in range(TQ // BLOCK):
            first_valid = jnp.where(t == 0, BLOCK - N_META, 0) if j == 0 else 0
            bias_t = _band_bias_t(first_valid)
            qb = qh[j * BLOCK:(j + 1) * BLOCK]
            acc = jnp.concatenate([_attn_piece(qb, kvar_ref, vt_ref, j * BLOCK, bias_t, sinks_ref, kvh, par)
                                   for par in range(2)], axis=0)
            blocks.append(jnp.concatenate([acc[:, i * BLOCK:(i + 1) * BLOCK].T for i in range(CHUNKS_PER_KV)], axis=1))
        z_attn = jnp.concatenate(blocks, axis=0) * _silu(ga_raw)
        z_ref[:, kvh * group:(kvh + 1) * group] = z_attn.astype(bf16)

    group = CHUNKS_PER_KV * LANES
    assert D_CONV == 2 * PROJ_PIECE and group == PROJ_PIECE and PROJ_PIECE == 2 * CONV_PIECE

    glu(0, proj(A0, PROJ_PIECE), proj(B0, PROJ_PIECE))
    a_hi = proj(A0 + PROJ_PIECE, PROJ_PIECE)
    conv = [conv_piece(0)]
    b_hi = proj(B0 + PROJ_PIECE, PROJ_PIECE)
    conv.append(conv_piece(1))
    glu(PROJ_PIECE, a_hi, b_hi)
    gc_lo = proj(GC0, PROJ_PIECE)
    conv.append(conv_piece(2))
    gc_hi = proj(GC0 + PROJ_PIECE, PROJ_PIECE)
    conv.append(conv_piece(3))
    q_lo = proj(Q0, group)
    z_conv = (_conv_ln(jnp.concatenate(conv, axis=1), bdw_ref, lng_ref, lnb_ref)
              * _silu(jnp.concatenate([gc_lo, gc_hi], axis=1)))
    z_ref[:, D_ATTN:D_ATTN + D_CONV] = z_conv.astype(bf16)

    q_hi = proj(Q0 + group, group)
    ga_lo = proj(GA0, group)
    attention(0, q_lo, ga_lo)
    ga_hi = proj(GA0 + group, group)
    o_conv = out_part(D_ATTN, D_CONV)
    attention(1, q_hi, ga_hi)
    o_lo = out_part(0, group)
    o_hi = out_part(group, group)

    hs, ssq = [], None
    for i, c in enumerate(range(0, D_MODEL, OUT_PIECE)):
        h = x_ref[0, :, c:c + OUT_PIECE] + (o_conv[i] + o_lo[i] + o_hi[i])
        part = jnp.sum(h * h, axis=-1, keepdims=True)
        ssq = part if ssq is None else ssq + part
        hs.append(h)
    scale = lax.rsqrt(ssq * (1.0 / D_MODEL) + EPS)
    for i, h in enumerate(hs):
        y_ref[0, :, i * OUT_PIECE:(i + 1) * OUT_PIECE] = h * scale * gf_ref[:, i * OUT_PIECE:(i + 1) * OUT_PIECE]

    @pl.when(t == nt - 1)
    def _():
        nk_ref[0] = k[TQ - WINDOW:TQ]
        nv_ref[0] = v[TQ - WINDOW:TQ]
        nc_ref[0] = glu_ref[PAD + TQ - PREFIX:PAD + TQ, :]

    kvar_ref[:, 0:BLOCK, :] = kvar_ref[:, TQ:TQ + BLOCK, :]
    vt_ref[:, :, 0:BLOCK] = vt_ref[:, :, TQ:TQ + BLOCK]
    glu_ref[0:PAD, :] = glu_ref[TQ:TQ + PAD, :]


def _prompt_call(sinks, x, tabs, p_meta, mtabs, g, w_in, w_dw, b_dw, ln_g, ln_b, w_out, gf):
    nb, seq, _ = x.shape
    nt = seq // TQ
    once = pl.Buffered(1)

    def const(shape):
        return pl.BlockSpec(shape, lambda b, t: (0,) * len(shape), pipeline_mode=once)

    q_scale = HEAD_DIM ** -0.5
    tab = jnp.concatenate(list(tabs) + [tb * q_scale for tb in tabs], axis=1)
    tab_spec = pl.BlockSpec((TQ, 6 * LANES), lambda b, t: (t, 0))
    return pl.pallas_call(
        _prompt_kernel,
        grid=(nb, nt),
        in_specs=[
            pl.BlockSpec(memory_space=pltpu.SMEM),
            pl.BlockSpec((1, TQ, D_MODEL), lambda b, t: (b, t, 0)),
            tab_spec,
            const((N_META, D_PROJ)), const((N_META, LANES)), const((N_META, LANES)), const((N_META, LANES)),
            const((1, D_MODEL)), const((D_MODEL, D_PROJ)),
            const((CONV_WIDTH, D_CONV)), const((1, D_CONV)), const((1, D_CONV)), const((1, D_CONV)),
            const((D_ATTN + D_CONV, D_MODEL)), const((1, D_MODEL)),
        ],
        out_specs=[
            pl.BlockSpec((1, TQ, D_MODEL), lambda b, t: (b, t, 0)),
            pl.BlockSpec((1, WINDOW, KV_WIDTH), lambda b, t: (b, 0, 0)),
            pl.BlockSpec((1, WINDOW, KV_WIDTH), lambda b, t: (b, 0, 0)),
            pl.BlockSpec((1, PREFIX, D_CONV), lambda b, t: (b, 0, 0)),
        ],
        out_shape=[
            jax.ShapeDtypeStruct((nb, seq, D_MODEL), f32),
            jax.ShapeDtypeStruct((nb, WINDOW, KV_WIDTH), f32),
            jax.ShapeDtypeStruct((nb, WINDOW, KV_WIDTH), f32),
            jax.ShapeDtypeStruct((nb, PREFIX, D_CONV), f32),
        ],
        scratch_shapes=[
            pltpu.VMEM((4, BLOCK + TQ, LANES), bf16),
            pltpu.VMEM((4, LANES, BLOCK + TQ), bf16),
            pltpu.VMEM((PAD + TQ, D_CONV), f32),
            pltpu.VMEM((TQ, D_MODEL), bf16),
            pltpu.VMEM((TQ, D_ATTN + D_CONV), bf16),
        ],
        compiler_params=pltpu.CompilerParams(
            dimension_semantics=("arbitrary", "arbitrary"), vmem_limit_bytes=VMEM_LIMIT),
    )(sinks, x, tab, p_meta, *mtabs, g, w_in, w_dw, b_dw, ln_g, ln_b, w_out, gf)


SEQ_BLOCK = 16
SEQ_UNROLL = 4
DEC = 8
KEYS = 2 * WINDOW


def _sample_mixer_kernel(p_ref, ck_ref, cv_ref, st_ref, cos_ref, sp_ref, sn_ref, sink_ref,
                         wdw_ref, bdw_ref, lng_ref, lnb_ref,
                         z_ref, nk_ref, nv_ref, nc_ref, q_scr, k_scr, v_scr, attn_scr, cbuf):
    rows = SEQ_BLOCK * DEC
    cos, sp, sn = cos_ref[...], sp_ref[...], sn_ref[...]

    k = _rope(p_ref[:, K0:K0 + KV_WIDTH], cos, sp, sn)
    v = p_ref[:, V0:V0 + KV_WIDTH]
    q_scr[...] = _rope(p_ref[:, Q0:Q0 + D_ATTN], cos, sp, sn) * (HEAD_DIM ** -0.5)
    k_scr[...] = k
    v_scr[...] = v

    nk_ref[:, 0:WINDOW - DEC, :] = ck_ref[:, DEC:WINDOW, :]
    nk_ref[:, WINDOW - DEC:WINDOW, :] = k.reshape(SEQ_BLOCK, DEC, KV_WIDTH)
    nv_ref[:, 0:WINDOW - DEC, :] = cv_ref[:, DEC:WINDOW, :]
    nv_ref[:, WINDOW - DEC:WINDOW, :] = v.reshape(SEQ_BLOCK, DEC, KV_WIDTH)

    glu = p_ref[:, A0:A0 + D_CONV] * _sigmoid(p_ref[:, B0:B0 + D_CONV])
    cbuf[:, 0:SUBLANES, :] = jnp.zeros((SEQ_BLOCK, SUBLANES, D_CONV), f32)
    cbuf[:, CONV_OFF:PAD, :] = st_ref[...]
    cbuf[:, PAD:PAD + DEC, :] = glu.reshape(SEQ_BLOCK, DEC, D_CONV)
    nc_ref[...] = cbuf[:, PAD + DEC - PREFIX:PAD + DEC, :]
    conv = _causal_conv(lambda s, j, lanes: cbuf[s, j * SUBLANES:(j + 1) * SUBLANES, lanes],
                        SEQ_BLOCK, DEC // SUBLANES, wdw_ref)
    z_conv = _conv_ln(conv, bdw_ref, lng_ref, lnb_ref) * _silu(p_ref[:, GC0:GC0 + D_CONV])

    row = lax.broadcasted_iota(jnp.int32, (N_HEADS * DEC, KEYS), 0) % DEC
    col = lax.broadcasted_iota(jnp.int32, (N_HEADS * DEC, KEYS), 1)
    valid = ((col < WINDOW) & (col >= row)) | ((col >= WINDOW) & (col - WINDOW <= row))
    bias = jnp.where(valid, 0.0, NEG).astype(f32)
    lo = lax.broadcasted_iota(jnp.int32, (DEC, LANES), 1) < HEAD_DIM
    zero = jnp.zeros((DEC, LANES), f32)
    pad_rows = jnp.zeros((KEYS - WINDOW - DEC, LANES), f32)
    sink = sink_ref[...]

    def body(s, carry):
        r0 = pl.multiple_of(s * DEC, DEC)
        qs = q_scr[pl.ds(r0, DEC), :]
        pieces = [None] * N_HEADS
        for c in range(CHUNKS):
            qc = qs[:, c * LANES:(c + 1) * LANES]
            qr = pltpu.roll(qc, HEAD_DIM, 1)
            if c < CHUNKS_PER_KV:
                pieces[2 * c], pieces[2 * c + 1] = jnp.where(lo, qc, zero), jnp.where(lo, qr, zero)
            else:
                pieces[2 * c], pieces[2 * c + 1] = jnp.where(lo, zero, qr), jnp.where(lo, zero, qc)
        qrows = jnp.concatenate(pieces, axis=0).astype(bf16)
        kf = jnp.concatenate([ck_ref[s], k_scr[pl.ds(r0, DEC), :], pad_rows], axis=0).astype(bf16)
        vf = jnp.concatenate([cv_ref[s], v_scr[pl.ds(r0, DEC), :], pad_rows], axis=0).astype(bf16)
        sc = lax.dot_general(qrows, kf, (((1,), (1,)), ((), ())), preferred_element_type=f32) + bias
        m = jnp.maximum(jnp.broadcast_to(jnp.max(sc, axis=1, keepdims=True), sink.shape), sink)
        e = jnp.exp(sc - jnp.concatenate([m] * (KEYS // LANES), axis=1))
        den = jnp.broadcast_to(jnp.sum(e, axis=1, keepdims=True), sink.shape) + jnp.exp(sink - m)
        o = jnp.dot(e.astype(bf16), vf, preferred_element_type=f32) * (1.0 / den)
        outs = []
        for c in range(CHUNKS):
            ev, od = o[2 * c * DEC:(2 * c + 1) * DEC], o[(2 * c + 1) * DEC:(2 * c + 2) * DEC]
            if c < CHUNKS_PER_KV:
                outs.append(jnp.where(lo, ev, pltpu.roll(od, HEAD_DIM, 1)))
            else:
                outs.append(jnp.where(lo, pltpu.roll(ev, HEAD_DIM, 1), od))
        attn_scr[pl.ds(r0, DEC), :] = jnp.concatenate(outs, axis=1)
        return carry

    lax.fori_loop(0, SEQ_BLOCK, body, 0, unroll=SEQ_UNROLL)

    z_attn = attn_scr[...] * _silu(p_ref[:, GA0:GA0 + D_ATTN])
    z_ref[:, 0:D_ATTN] = z_attn.astype(bf16)
    z_ref[:, D_ATTN:D_ATTN + D_CONV] = z_conv.astype(bf16)


def _sample_mixer(p, ck, cv, st, tabs, sink_col, w_dw, b_dw, ln_g, ln_b):
    nseq = ck.shape[0]
    rows = SEQ_BLOCK * DEC

    def const(shape):
        return pl.BlockSpec(shape, lambda i: (0,) * len(shape))

    return pl.pallas_call(
        _sample_mixer_kernel,
        grid=(nseq // SEQ_BLOCK,),
        in_specs=[
            pl.BlockSpec((rows, D_PROJ), lambda i: (i, 0)),
            pl.BlockSpec((SEQ_BLOCK, WINDOW, KV_WIDTH), lambda i: (i, 0, 0)),
            pl.BlockSpec((SEQ_BLOCK, WINDOW, KV_WIDTH), lambda i: (i, 0, 0)),
            pl.BlockSpec((SEQ_BLOCK, PREFIX, D_CONV), lambda i: (i, 0, 0)),
            const((rows, LANES)), const((rows, LANES)), const((rows, LANES)),
            const((N_HEADS * DEC, LANES)),
            const((CONV_WIDTH, D_CONV)), const((1, D_CONV)), const((1, D_CONV)), const((1, D_CONV)),
        ],
        out_specs=[
            pl.BlockSpec((rows, D_ATTN + D_CONV), lambda i: (i, 0)),
            pl.BlockSpec((SEQ_BLOCK, WINDOW, KV_WIDTH), lambda i: (i, 0, 0)),
            pl.BlockSpec((SEQ_BLOCK, WINDOW, KV_WIDTH), lambda i: (i, 0, 0)),
            pl.BlockSpec((SEQ_BLOCK, PREFIX, D_CONV), lambda i: (i, 0, 0)),
        ],
        out_shape=[
            jax.ShapeDtypeStruct((nseq * DEC, D_ATTN + D_CONV), bf16),
            jax.ShapeDtypeStruct((nseq, WINDOW, KV_WIDTH), f32),
            jax.ShapeDtypeStruct((nseq, WINDOW, KV_WIDTH), f32),
            jax.ShapeDtypeStruct((nseq, PREFIX, D_CONV), f32),
        ],
        scratch_shapes=[
            pltpu.VMEM((rows, D_ATTN), f32),
            pltpu.VMEM((rows, KV_WIDTH), f32),
            pltpu.VMEM((rows, KV_WIDTH), f32),
            pltpu.VMEM((rows, D_ATTN), f32),
            pltpu.VMEM((SEQ_BLOCK, PAD + DEC, D_CONV), f32),
        ],
        compiler_params=pltpu.CompilerParams(dimension_semantics=("arbitrary",), vmem_limit_bytes=VMEM_LIMIT),
    )(p, ck, cv, st, *tabs, sink_col, w_dw, b_dw, ln_g, ln_b)


OUT_BM = 512


def _out_kernel(z_ref, x_ref, w_ref, gf_ref, y_ref):
    h = x_ref[...] + jnp.dot(z_ref[...], w_ref[...], preferred_element_type=f32)
    y_ref[...] = _rms_norm(h, gf_ref[...])


def _out_project(z, x, w_out, gf):
    n = x.shape[0]
    return pl.pallas_call(
        _out_kernel,
        grid=(n // OUT_BM,),
        in_specs=[
            pl.BlockSpec((OUT_BM, D_ATTN + D_CONV), lambda i: (i, 0)),
            pl.BlockSpec((OUT_BM, D_MODEL), lambda i: (i, 0)),
            pl.BlockSpec((D_ATTN + D_CONV, D_MODEL), lambda i: (0, 0), pipeline_mode=pl.Buffered(1)),
            pl.BlockSpec((1, D_MODEL), lambda i: (0, 0)),
        ],
        out_specs=pl.BlockSpec((OUT_BM, D_MODEL), lambda i: (i, 0)),
        out_shape=jax.ShapeDtypeStruct((n, D_MODEL), f32),
        compiler_params=pltpu.CompilerParams(dimension_semantics=("arbitrary",), vmem_limit_bytes=VMEM_LIMIT),
    )(z, x, w_out, gf)


def kernel(x_prompt, x_sample, cache_k_win, cache_v_win, state_conv, meta_tokens, norm_gain, w_in,
           attn_sinks, w_dw, b_dw, conv_norm_gain, conv_norm_bias, w_out, final_norm_gain):
    nb, seq, _ = x_prompt.shape
    nseq, dec, _ = x_sample.shape
    assert norm_gain.shape[0] == 1 and dec == DEC and seq % TQ == 0 and nseq % SEQ_BLOCK == 0

    g = norm_gain[0][None]
    w_out_b = w_out[0].astype(bf16)
    sinks = attn_sinks[0]
    wdw, bdw = w_dw[0], b_dw[0][None]
    lng, lnb = conv_norm_gain[0][None], conv_norm_bias[0][None]
    gf = final_norm_gain[None]

    xs = x_sample.reshape(nseq * dec, D_MODEL)
    p_s, p_meta, w_in_b = _project_sample(xs, meta_tokens, g, w_in[0])

    tabs = _rope_tables(N_META + jnp.arange(seq))
    mtabs = _rope_tables(jnp.arange(N_META))
    y_prompt, nk_p, nv_p, nc_p = _prompt_call(sinks, x_prompt, tabs, p_meta, mtabs, g, w_in_b,
                                               wdw, bdw, lng, lnb, w_out_b, gf)

    stabs = _rope_tables(PAST_LEN + jnp.arange(SEQ_BLOCK * dec) % dec)
    sink_col = jnp.broadcast_to(jnp.repeat(sinks, dec)[:, None], (N_HEADS * dec, LANES))
    ck = cache_k_win[0].reshape(nseq, WINDOW, KV_WIDTH)
    cv = cache_v_win[0].reshape(nseq, WINDOW, KV_WIDTH)
    z_s, nk_s, nv_s, nc_s = _sample_mixer(p_s, ck, cv, state_conv[0], stabs, sink_col, wdw, bdw, lng, lnb)
    y_sample = _out_project(z_s, xs, w_out_b, gf).reshape(nseq, dec, D_MODEL)

    kv_shape = (1, -1, WINDOW, N_KV_HEADS, HEAD_DIM)
    return (y_prompt, y_sample, nk_p.reshape(kv_shape), nv_p.reshape(kv_shape), nc_p[None],
            nk_s.reshape(kv_shape), nv_s.reshape(kv_shape), nc_s[None])
```

```python
import jax
import jax.numpy as jnp
from jax import lax
from jax.experimental import pallas as pl
from jax.experimental.pallas import tpu as pltpu

D_MODEL = 2048
N_META = 16
D_ATTN = 1024
D_CONV = 1024
HEAD_DIM = 64
N_HEADS = 16
N_KV_HEADS = 2
KV_WIDTH = N_KV_HEADS * HEAD_DIM
WINDOW = 128
BLOCK = 128
ROPE_DIM = 16
ROPE_THETA = 500000.0
CONV_WIDTH = 31
PREFIX = CONV_WIDTH - 1
D_PROJ = 2 * D_ATTN + 2 * KV_WIDTH + 3 * D_CONV
EPS = 1e-5
PAST_LEN = 8192

Q0 = 0
K0 = D_ATTN
V0 = K0 + KV_WIDTH
GA0 = V0 + KV_WIDTH
A0 = GA0 + D_ATTN
B0 = A0 + D_CONV
GC0 = B0 + D_CONV

LANES = 128
SUBLANES = 8
CHUNKS = D_ATTN // LANES
CHUNKS_PER_KV = CHUNKS // N_KV_HEADS
PAD = 32
NEG = -1e30

TQ = 256
VMEM_LIMIT = 56 * 1024 * 1024

f32 = jnp.float32
bf16 = jnp.bfloat16


def _rms_norm(x, g):
    return x * lax.rsqrt(jnp.mean(x * x, axis=-1, keepdims=True) + EPS) * g


NEG_LOG2_E = -1.4426950408889634


def _sigmoid(x):
    return 1.0 / (1.0 + jnp.exp2(x * NEG_LOG2_E))


def _silu(x):
    return x * _sigmoid(x)


def _rope(x, cos, sin_prev, sin_next):
    outs = []
    for c in range(x.shape[1] // LANES):
        xc = x[:, c * LANES:(c + 1) * LANES]
        outs.append(xc * cos + pltpu.roll(xc, 8, 1) * sin_prev + pltpu.roll(xc, LANES - 8, 1) * sin_next)
    return outs[0] if len(outs) == 1 else jnp.concatenate(outs, axis=1)


def _half_variants(x):
    lo = lax.broadcasted_iota(jnp.int32, x.shape, 1) < HEAD_DIM
    xr = pltpu.roll(x, HEAD_DIM, 1)
    zero = jnp.zeros_like(x)
    return (jnp.where(lo, x, zero).astype(bf16), jnp.where(lo, zero, xr).astype(bf16),
            jnp.where(lo, xr, zero).astype(bf16), jnp.where(lo, zero, x).astype(bf16))


def _value_variants_t(v):
    vt = v.T
    top, bot = vt[0:HEAD_DIM], vt[HEAD_DIM:2 * HEAD_DIM]
    ones = jnp.ones_like(top)
    return tuple(jnp.concatenate(pair, axis=0).astype(bf16)
                 for pair in ((top, ones), (ones, top), (bot, ones), (ones, bot)))


def _layer_norm(x, g, b):
    mu = jnp.mean(x, axis=-1, keepdims=True)
    xc = x - mu
    return xc * lax.rsqrt(jnp.mean(xc * xc, axis=-1, keepdims=True) + EPS) * g + b


def _rope_tables(pos):
    half = ROPE_DIM // 2
    inv = ROPE_THETA ** (-jnp.arange(0, ROPE_DIM, 2, dtype=f32) / ROPE_DIM)
    ang = pos.astype(f32)[:, None] * inv[None, :]
    cos, sin = jnp.cos(ang), jnp.sin(ang)
    t = pos.shape[0]
    ones = jnp.ones((t, HEAD_DIM - ROPE_DIM), f32)
    zeros = jnp.zeros((t, HEAD_DIM - ROPE_DIM), f32)
    zh = jnp.zeros((t, half), f32)
    cos64 = jnp.concatenate([cos, cos, ones], axis=1)
    prev64 = jnp.concatenate([zh, sin, zeros], axis=1)
    next64 = jnp.concatenate([-sin, zh, zeros], axis=1)
    rep = LANES // HEAD_DIM
    return jnp.tile(cos64, (1, rep)), jnp.tile(prev64, (1, rep)), jnp.tile(next64, (1, rep))


PROJ_BN = 768


def _proj_kernel(x_ref, meta_ref, g_ref, w_ref, p_ref, pm_ref, wb_ref, u_ref):
    n = x_ref.shape[0]

    @pl.when(pl.program_id(0) == 0)
    def _():
        u_ref[0:n, :] = _rms_norm(x_ref[...], g_ref[...]).astype(bf16)
        u_ref[n:n + N_META, :] = _rms_norm(meta_ref[...], g_ref[...]).astype(bf16)

    w = w_ref[...].astype(bf16)
    wb_ref[...] = w
    p = jnp.dot(u_ref[...], w, preferred_element_type=f32)
    p_ref[...] = p[0:n]
    pm_ref[...] = p[n:n + N_META]


def _project_sample(x, meta, g, w_in):
    n = x.shape[0]
    once = pl.Buffered(1)
    return pl.pallas_call(
        _proj_kernel,
        grid=(D_PROJ // PROJ_BN,),
        in_specs=[
            pl.BlockSpec((n, D_MODEL), lambda j: (0, 0), pipeline_mode=once),
            pl.BlockSpec((N_META, D_MODEL), lambda j: (0, 0), pipeline_mode=once),
            pl.BlockSpec((1, D_MODEL), lambda j: (0, 0), pipeline_mode=once),
            pl.BlockSpec((D_MODEL, PROJ_BN), lambda j: (0, j)),
        ],
        out_specs=[
            pl.BlockSpec((n, PROJ_BN), lambda j: (0, j)),
            pl.BlockSpec((N_META, PROJ_BN), lambda j: (0, j)),
            pl.BlockSpec((D_MODEL, PROJ_BN), lambda j: (0, j)),
        ],
        out_shape=[jax.ShapeDtypeStruct((n, D_PROJ), f32), jax.ShapeDtypeStruct((N_META, D_PROJ), f32),
                   jax.ShapeDtypeStruct((D_MODEL, D_PROJ), bf16)],
        scratch_shapes=[pltpu.VMEM((n + N_META, D_MODEL), bf16)],
        compiler_params=pltpu.CompilerParams(dimension_semantics=("arbitrary",), vmem_limit_bytes=VMEM_LIMIT),
    )(x, meta, g, w_in)


CONV_OFF = PAD - PREFIX
CONV_TAP_TILES = (CONV_OFF + CONV_WIDTH - 1) // SUBLANES + 1


def _conv_chunk(load_tile, n_seq, n_tiles, wdw_ref, c):
    row = lax.broadcasted_iota(jnp.int32, (SUBLANES, LANES), 0)
    lanes = slice(c * LANES, (c + 1) * LANES)
    w = [jnp.broadcast_to(wdw_ref[k:k + 1, lanes], (SUBLANES, LANES)) for k in range(CONV_WIDTH)]
    outs = []
    for s in range(n_seq):
        g = [load_tile(s, j, lanes) for j in range(n_tiles + CONV_TAP_TILES - 1)]

        def z(r, i):
            acc = None
            for a in range(CONV_TAP_TILES):
                k = SUBLANES * a + r - CONV_OFF
                if 0 <= k < CONV_WIDTH:
                    term = w[k] * g[i + a]
                    acc = term if acc is None else acc + term
            return acc

        prev = [z(r, 0) for r in range(1, SUBLANES)]
        for i in range(n_tiles):
            nxt = [z(r, i + 1) for r in range(1, SUBLANES)]
            y = z(0, i)
            for r in range(1, SUBLANES):
                y = y + pltpu.roll(jnp.where(row >= r, prev[r - 1], nxt[r - 1]), SUBLANES - r, 0)
            outs.append(y)
            prev = nxt
    return jnp.concatenate(outs, axis=0)


def _causal_conv(load_tile, n_seq, n_tiles, wdw_ref):
    return jnp.concatenate([_conv_chunk(load_tile, n_seq, n_tiles, wdw_ref, c)
                            for c in range(D_CONV // LANES)], axis=1)


def _conv_ln(y, bdw_ref, lng_ref, lnb_ref):
    return _silu(_layer_norm(y + bdw_ref[...], lng_ref[...], lnb_ref[...]))


def _band_bias_t(first_valid_key):
    key = lax.broadcasted_iota(jnp.int32, (2 * BLOCK, BLOCK), 0)
    qry = lax.broadcasted_iota(jnp.int32, (2 * BLOCK, BLOCK), 1)
    valid = (key >= qry) & (key <= qry + WINDOW) & (key >= first_valid_key)
    return jnp.where(valid, 0.0, NEG).astype(f32)


def _attn_piece(q_blk, kvar_ref, vt_ref, row0, bias_t, sinks_ref, kv, par):
    qs = jnp.concatenate([q_blk[:, i * LANES:(i + 1) * LANES] for i in range(CHUNKS_PER_KV)], axis=0)
    kk = kvar_ref[kv * 2 + par, row0:row0 + 2 * BLOCK, :]
    vt = vt_ref[kv * 2 + par, :, row0:row0 + 2 * BLOCK]
    st = lax.dot_general(kk, qs, (((1,), (1,)), ((), ())), preferred_element_type=f32)
    es, sink_terms = [], []
    for i in range(CHUNKS_PER_KV):
        blk = st[:, i * BLOCK:(i + 1) * BLOCK] + bias_t
        sink = sinks_ref[kv * 8 + 2 * i + par]
        m = jnp.maximum(jnp.max(blk, axis=0, keepdims=True), sink)
        es.append(jnp.exp(blk - m).astype(bf16))
        sink_terms.append(jnp.exp(sink - m))
    out_t = jnp.dot(vt, jnp.concatenate(es, axis=1), preferred_element_type=f32)
    vals, sums = (0, HEAD_DIM) if par == 0 else (HEAD_DIM, 0)
    den = out_t[sums:sums + 1, :] + jnp.concatenate(sink_terms, axis=1)
    return out_t[vals:vals + HEAD_DIM, :] * (1.0 / den)


PROJ_PIECE = 512
OUT_PIECE = 512
CONV_PIECE = 256


def _prompt_kernel(sinks_ref, x_ref, tab_ref, pm_ref, mcos_ref, msp_ref, msn_ref,
                   g_ref, win_ref, wdw_ref, bdw_ref, lng_ref, lnb_ref, wout_ref, gf_ref,
                   y_ref, nk_ref, nv_ref, nc_ref, kvar_ref, vt_ref, glu_ref, u_ref, z_ref):
    t = pl.program_id(1)
    nt = pl.num_programs(1)

    @pl.when(t == 0)
    def _():
        pm = pm_ref[...]
        km = _rope(pm[:, K0:K0 + KV_WIDTH], mcos_ref[...], msp_ref[...], msn_ref[...])
        vm = pm[:, V0:V0 + KV_WIDTH]
        invalid = jnp.zeros((BLOCK - N_META, LANES), f32)
        kvar_ref[:, 0:BLOCK - N_META, :] = jnp.zeros((4, BLOCK - N_META, LANES), bf16)
        for i, var in enumerate(_half_variants(km)):
            kvar_ref[i, BLOCK - N_META:BLOCK, :] = var
        for i, var in enumerate(_value_variants_t(jnp.concatenate([invalid, vm], axis=0))):
            vt_ref[i, :, 0:BLOCK] = var
        glu_ref[0:PAD - N_META, :] = jnp.zeros((PAD - N_META, D_CONV), f32)
        glu_ref[PAD - N_META:PAD, :] = pm[:, A0:A0 + D_CONV] * _sigmoid(pm[:, B0:B0 + D_CONV])

    u_ref[...] = _rms_norm(x_ref[0], g_ref[...]).astype(bf16)

    def proj(c0, width):
        return jnp.dot(u_ref[...], win_ref[:, c0:c0 + width], preferred_element_type=f32)

    cos, sp, sn, cos_q, sp_q, sn_q = (tab_ref[:, i * LANES:(i + 1) * LANES] for i in range(6))

    kv = proj(K0, 2 * KV_WIDTH)
    k = _rope(kv[:, 0:KV_WIDTH], cos, sp, sn)
    v = kv[:, KV_WIDTH:2 * KV_WIDTH]
    for i, var in enumerate(_half_variants(k)):
        kvar_ref[i, BLOCK:BLOCK + TQ, :] = var
    for j in range(TQ // BLOCK):
        for i, var in enumerate(_value_variants_t(v[j * BLOCK:(j + 1) * BLOCK])):
            vt_ref[i, :, (j + 1) * BLOCK:(j + 2) * BLOCK] = var

    def load_tile(s, i, lanes):
        return glu_ref[i * SUBLANES:(i + 1) * SUBLANES, lanes]

    def glu(c0, a, b):
        glu_ref[PAD:PAD + TQ, c0:c0 + PROJ_PIECE] = a * _sigmoid(b)

    def conv_piece(p):
        first = p * CONV_PIECE // LANES
        return jnp.concatenate([_conv_chunk(load_tile, 1, TQ // SUBLANES, wdw_ref, c)
                                for c in range(first, first + CONV_PIECE // LANES)], axis=1)

    def out_part(k0, width):
        lhs = z_ref[:, k0:k0 + width]
        return [jnp.dot(lhs, wout_ref[k0:k0 + width, c:c + OUT_PIECE], preferred_element_type=f32)
                for c in range(0, D_MODEL, OUT_PIECE)]

    def attention(kvh, q_raw, ga_raw):
        qh = _rope(q_raw, cos_q, sp_q, sn_q).astype(bf16)
        blocks = []
        for j in range(TQ // BLOCK):
            first_valid = jnp.where(t == 0, BLOCK - N_META, 0) if j == 0 else 0
            bias_t = _band_bias_t(first_valid)
            qb = qh[j * BLOCK:(j + 1) * BLOCK]
            acc = jnp.concatenate([_attn_piece(qb, kvar_ref, vt_ref, j * BLOCK, bias_t, sinks_ref, kvh, par)
                                   for par in range(2)], axis=0)
            blocks.append(jnp.concatenate([acc[:, i * BLOCK:(i + 1) * BLOCK].T for i in range(CHUNKS_PER_KV)], axis=1))
        z_attn = jnp.concatenate(blocks, axis=0) * _silu(ga_raw)
        z_ref[:, kvh * group:(kvh + 1) * group] = z_attn.astype(bf16)

    group = CHUNKS_PER_KV * LANES
    assert D_CONV == 2 * PROJ_PIECE and group == PROJ_PIECE and PROJ_PIECE == 2 * CONV_PIECE

    glu(0, proj(A0, PROJ_PIECE), proj(B0, PROJ_PIECE))
    a_hi = proj(A0 + PROJ_PIECE, PROJ_PIECE)
    conv = [conv_piece(0)]
    b_hi = proj(B0 + PROJ_PIECE, PROJ_PIECE)
    conv.append(conv_piece(1))
    glu(PROJ_PIECE, a_hi, b_hi)
    gc_lo = proj(GC0, PROJ_PIECE)
    conv.append(conv_piece(2))
    gc_hi = proj(GC0 + PROJ_PIECE, PROJ_PIECE)
    conv.append(conv_piece(3))
    q_lo = proj(Q0, group)
    z_conv = (_conv_ln(jnp.concatenate(conv, axis=1), bdw_ref, lng_ref, lnb_ref)
              * _silu(jnp.concatenate([gc_lo, gc_hi], axis=1)))
    z_ref[:, D_ATTN:D_ATTN + D_CONV] = z_conv.astype(bf16)

    q_hi = proj(Q0 + group, group)
    ga_lo = proj(GA0, group)
    attention(0, q_lo, ga_lo)
    ga_hi = proj(GA0 + group, group)
    o_conv = out_part(D_ATTN, D_CONV)
    attention(1, q_hi, ga_hi)
    o_lo = out_part(0, group)
    o_hi = out_part(group, group)

    hs, ssq = [], None
    for i, c in enumerate(range(0, D_MODEL, OUT_PIECE)):
        h = x_ref[0, :, c:c + OUT_PIECE] + (o_conv[i] + o_lo[i] + o_hi[i])
        part = jnp.sum(h * h, axis=-1, keepdims=True)
        ssq = part if ssq is None else ssq + part
        hs.append(h)
    scale = lax.rsqrt(ssq * (1.0 / D_MODEL) + EPS)
    for i, h in enumerate(hs):
        y_ref[0, :, i * OUT_PIECE:(i + 1) * OUT_PIECE] = h * scale * gf_ref[:, i * OUT_PIECE:(i + 1) * OUT_PIECE]

    @pl.when(t == nt - 1)
    def _():
        nk_ref[0] = k[TQ - WINDOW:TQ]
        nv_ref[0] = v[TQ - WINDOW:TQ]
        nc_ref[0] = glu_ref[PAD + TQ - PREFIX:PAD + TQ, :]

    kvar_ref[:, 0:BLOCK, :] = kvar_ref[:, TQ:TQ + BLOCK, :]
    vt_ref[:, :, 0:BLOCK] = vt_ref[:, :, TQ:TQ + BLOCK]
    glu_ref[0:PAD, :] = glu_ref[TQ:TQ + PAD, :]


def _prompt_call(sinks, x, tabs, p_meta, mtabs, g, w_in, w_dw, b_dw, ln_g, ln_b, w_out, gf):
    nb, seq, _ = x.shape
    nt = seq // TQ
    once = pl.Buffered(1)

    def const(shape):
        return pl.BlockSpec(shape, lambda b, t: (0,) * len(shape), pipeline_mode=once)

    q_scale = HEAD_DIM ** -0.5
    tab = jnp.concatenate(list(tabs) + [tb * q_scale for tb in tabs], axis=1)
    tab_spec = pl.BlockSpec((TQ, 6 * LANES), lambda b, t: (t, 0))
    return pl.pallas_call(
        _prompt_kernel,
        grid=(nb, nt),
        in_specs=[
            pl.BlockSpec(memory_space=pltpu.SMEM),
            pl.BlockSpec((1, TQ, D_MODEL), lambda b, t: (b, t, 0)),
            tab_spec,
            const((N_META, D_PROJ)), const((N_META, LANES)), const((N_META, LANES)), const((N_META, LANES)),
            const((1, D_MODEL)), const((D_MODEL, D_PROJ)),
            const((CONV_WIDTH, D_CONV)), const((1, D_CONV)), const((1, D_CONV)), const((1, D_CONV)),
            const((D_ATTN + D_CONV, D_MODEL)), const((1, D_MODEL)),
        ],
        out_specs=[
            pl.BlockSpec((1, TQ, D_MODEL), lambda b, t: (b, t, 0)),
            pl.BlockSpec((1, WINDOW, KV_WIDTH), lambda b, t: (b, 0, 0)),
            pl.BlockSpec((1, WINDOW, KV_WIDTH), lambda b, t: (b, 0, 0)),
            pl.BlockSpec((1, PREFIX, D_CONV), lambda b, t: (b, 0, 0)),
        ],
        out_shape=[
            jax.ShapeDtypeStruct((nb, seq, D_MODEL), f32),
            jax.ShapeDtypeStruct((nb, WINDOW, KV_WIDTH), f32),
            jax.ShapeDtypeStruct((nb, WINDOW, KV_WIDTH), f32),
            jax.ShapeDtypeStruct((nb, PREFIX, D_CONV), f32),
        ],
        scratch_shapes=[
            pltpu.VMEM((4, BLOCK + TQ, LANES), bf16),
            pltpu.VMEM((4, LANES, BLOCK + TQ), bf16),
            pltpu.VMEM((PAD + TQ, D_CONV), f32),
            pltpu.VMEM((TQ, D_MODEL), bf16),
            pltpu.VMEM((TQ, D_ATTN + D_CONV), bf16),
        ],
        compiler_params=pltpu.CompilerParams(
            dimension_semantics=("arbitrary", "arbitrary"), vmem_limit_bytes=VMEM_LIMIT),
    )(sinks, x, tab, p_meta, *mtabs, g, w_in, w_dw, b_dw, ln_g, ln_b, w_out, gf)


SEQ_BLOCK = 16
SEQ_UNROLL = 4
DEC = 8
KEYS = 2 * WINDOW


def _sample_mixer_kernel(p_ref, x_ref, ck_ref, cv_ref, st_ref, cos_ref, sp_ref, sn_ref, sink_ref,
                         wdw_ref, bdw_ref, lng_ref, lnb_ref, wout_ref, gf_ref,
                         y_ref, nk_ref, nv_ref, nc_ref, q_scr, k_scr, v_scr, attn_scr, cbuf):
    rows = SEQ_BLOCK * DEC
    cos, sp, sn = cos_ref[...], sp_ref[...], sn_ref[...]

    k = _rope(p_ref[:, K0:K0 + KV_WIDTH], cos, sp, sn)
    v = p_ref[:, V0:V0 + KV_WIDTH]
    q_scr[...] = _rope(p_ref[:, Q0:Q0 + D_ATTN], cos, sp, sn) * (HEAD_DIM ** -0.5)
    k_scr[...] = k
    v_scr[...] = v

    nk_ref[:, 0:WINDOW - DEC, :] = ck_ref[:, DEC:WINDOW, :]
    nk_ref[:, WINDOW - DEC:WINDOW, :] = k.reshape(SEQ_BLOCK, DEC, KV_WIDTH)
    nv_ref[:, 0:WINDOW - DEC, :] = cv_ref[:, DEC:WINDOW, :]
    nv_ref[:, WINDOW - DEC:WINDOW, :] = v.reshape(SEQ_BLOCK, DEC, KV_WIDTH)

    glu = p_ref[:, A0:A0 + D_CONV] * _sigmoid(p_ref[:, B0:B0 + D_CONV])
    cbuf[:, 0:SUBLANES, :] = jnp.zeros((SEQ_BLOCK, SUBLANES, D_CONV), f32)
    cbuf[:, CONV_OFF:PAD, :] = st_ref[...]
    cbuf[:, PAD:PAD + DEC, :] = glu.reshape(SEQ_BLOCK, DEC, D_CONV)
    nc_ref[...] = cbuf[:, PAD + DEC - PREFIX:PAD + DEC, :]
    conv = _causal_conv(lambda s, j, lanes: cbuf[s, j * SUBLANES:(j + 1) * SUBLANES, lanes],
                        SEQ_BLOCK, DEC // SUBLANES, wdw_ref)
    z_conv = _conv_ln(conv, bdw_ref, lng_ref, lnb_ref) * _silu(p_ref[:, GC0:GC0 + D_CONV])
    o_conv = jnp.dot(z_conv.astype(bf16), wout_ref[D_ATTN:D_ATTN + D_CONV, :], preferred_element_type=f32)

    row = lax.broadcasted_iota(jnp.int32, (N_HEADS * DEC, KEYS), 0) % DEC
    col = lax.broadcasted_iota(jnp.int32, (N_HEADS * DEC, KEYS), 1)
    valid = ((col < WINDOW) & (col >= row)) | ((col >= WINDOW) & (col - WINDOW <= row))
    bias = jnp.where(valid, 0.0, NEG).astype(f32)
    lo = lax.broadcasted_iota(jnp.int32, (DEC, LANES), 1) < HEAD_DIM
    zero = jnp.zeros((DEC, LANES), f32)
    pad_rows = jnp.zeros((KEYS - WINDOW - DEC, LANES), f32)
    sink = sink_ref[...]

    def body(s, carry):
        r0 = pl.multiple_of(s * DEC, DEC)
        qs = q_scr[pl.ds(r0, DEC), :]
        pieces = [None] * N_HEADS
        for c in range(CHUNKS):
            qc = qs[:, c * LANES:(c + 1) * LANES]
            qr = pltpu.roll(qc, HEAD_DIM, 1)
            if c < CHUNKS_PER_KV:
                pieces[2 * c], pieces[2 * c + 1] = jnp.where(lo, qc, zero), jnp.where(lo, qr, zero)
            else:
                pieces[2 * c], pieces[2 * c + 1] = jnp.where(lo, zero, qr), jnp.where(lo, zero, qc)
        qrows = jnp.concatenate(pieces, axis=0).astype(bf16)
        kf = jnp.concatenate([ck_ref[s], k_scr[pl.ds(r0, DEC), :], pad_rows], axis=0).astype(bf16)
        vf = jnp.concatenate([cv_ref[s], v_scr[pl.ds(r0, DEC), :], pad_rows], axis=0).astype(bf16)
        sc = lax.dot_general(qrows, kf, (((1,), (1,)), ((), ())), preferred_element_type=f32) + bias
        m = jnp.maximum(jnp.broadcast_to(jnp.max(sc, axis=1, keepdims=True), sink.shape), sink)
        e = jnp.exp(sc - jnp.concatenate([m] * (KEYS // LANES), axis=1))
        den = jnp.broadcast_to(jnp.sum(e, axis=1, keepdims=True), sink.shape) + jnp.exp(sink - m)
        o = jnp.dot(e.astype(bf16), vf, preferred_element_type=f32) * (1.0 / den)
        outs = []
        for c in range(CHUNKS):
            ev, od = o[2 * c * DEC:(2 * c + 1) * DEC], o[(2 * c + 1) * DEC:(2 * c + 2) * DEC]
            if c < CHUNKS_PER_KV:
                outs.append(jnp.where(lo, ev, pltpu.roll(od, HEAD_DIM, 1)))
            else:
                outs.append(jnp.where(lo, pltpu.roll(ev, HEAD_DIM, 1), od))
        attn_scr[pl.ds(r0, DEC), :] = jnp.concatenate(outs, axis=1)
        return carry

    lax.fori_loop(0, SEQ_BLOCK, body, 0, unroll=SEQ_UNROLL)

    z_attn = attn_scr[...] * _silu(p_ref[:, GA0:GA0 + D_ATTN])
    o_attn = jnp.dot(z_attn.astype(bf16), wout_ref[0:D_ATTN, :], preferred_element_type=f32)
    y_ref[...] = _rms_norm(x_ref[...] + (o_conv + o_attn), gf_ref[...])


def _sample_mixer(p, x, ck, cv, st, tabs, sink_col, w_dw, b_dw, ln_g, ln_b, w_out, gf):
    nseq = ck.shape[0]
    rows = SEQ_BLOCK * DEC

    def const(shape):
        return pl.BlockSpec(shape, lambda i: (0,) * len(shape))

    return pl.pallas_call(
        _sample_mixer_kernel,
        grid=(nseq // SEQ_BLOCK,),
        in_specs=[
            pl.BlockSpec((rows, D_PROJ), lambda i: (i, 0)),
            pl.BlockSpec((rows, D_MODEL), lambda i: (i, 0)),
            pl.BlockSpec((SEQ_BLOCK, WINDOW, KV_WIDTH), lambda i: (i, 0, 0)),
            pl.BlockSpec((SEQ_BLOCK, WINDOW, KV_WIDTH), lambda i: (i, 0, 0)),
            pl.BlockSpec((SEQ_BLOCK, PREFIX, D_CONV), lambda i: (i, 0, 0)),
            const((rows, LANES)), const((rows, LANES)), const((rows, LANES)),
            const((N_HEADS * DEC, LANES)),
            const((CONV_WIDTH, D_CONV)), const((1, D_CONV)), const((1, D_CONV)), const((1, D_CONV)),
            pl.BlockSpec((D_ATTN + D_CONV, D_MODEL), lambda i: (0, 0), pipeline_mode=pl.Buffered(1)),
            const((1, D_MODEL)),
        ],
        out_specs=[
            pl.BlockSpec((rows, D_MODEL), lambda i: (i, 0)),
            pl.BlockSpec((SEQ_BLOCK, WINDOW, KV_WIDTH), lambda i: (i, 0, 0)),
            pl.BlockSpec((SEQ_BLOCK, WINDOW, KV_WIDTH), lambda i: (i, 0, 0)),
            pl.BlockSpec((SEQ_BLOCK, PREFIX, D_CONV), lambda i: (i, 0, 0)),
        ],
        out_shape=[
            jax.ShapeDtypeStruct((nseq * DEC, D_MODEL), f32),
            jax.ShapeDtypeStruct((nseq, WINDOW, KV_WIDTH), f32),
            jax.ShapeDtypeStruct((nseq, WINDOW, KV_WIDTH), f32),
            jax.ShapeDtypeStruct((nseq, PREFIX, D_CONV), f32),
        ],
        scratch_shapes=[
            pltpu.VMEM((rows, D_ATTN), f32),
            pltpu.VMEM((rows, KV_WIDTH), f32),
            pltpu.VMEM((rows, KV_WIDTH), f32),
            pltpu.VMEM((rows, D_ATTN), f32),
            pltpu.VMEM((SEQ_BLOCK, PAD + DEC, D_CONV), f32),
        ],
        compiler_params=pltpu.CompilerParams(dimension_semantics=("arbitrary",), vmem_limit_bytes=VMEM_LIMIT),
    )(p, x, ck, cv, st, *tabs, sink_col, w_dw, b_dw, ln_g, ln_b, w_out, gf)


def kernel(x_prompt, x_sample, cache_k_win, cache_v_win, state_conv, meta_tokens, norm_gain, w_in,
           attn_sinks, w_dw, b_dw, conv_norm_gain, conv_norm_bias, w_out, final_norm_gain):
    nb, seq, _ = x_prompt.shape
    nseq, dec, _ = x_sample.shape
    assert norm_gain.shape[0] == 1 and dec == DEC and seq % TQ == 0 and nseq % SEQ_BLOCK == 0

    g = norm_gain[0][None]
    w_out_b = w_out[0].astype(bf16)
    sinks = attn_sinks[0]
    wdw, bdw = w_dw[0], b_dw[0][None]
    lng, lnb = conv_norm_gain[0][None], conv_norm_bias[0][None]
    gf = final_norm_gain[None]

    xs = x_sample.reshape(nseq * dec, D_MODEL)
    p_s, p_meta, w_in_b = _project_sample(xs, meta_tokens, g, w_in[0])

    tabs = _rope_tables(N_META + jnp.arange(seq))
    mtabs = _rope_tables(jnp.arange(N_META))
    y_prompt, nk_p, nv_p, nc_p = _prompt_call(sinks, x_prompt, tabs, p_meta, mtabs, g, w_in_b,
                                               wdw, bdw, lng, lnb, w_out_b, gf)

    stabs = _rope_tables(PAST_LEN + jnp.arange(SEQ_BLOCK * dec) % dec)
    sink_col = jnp.broadcast_to(jnp.repeat(sinks, dec)[:, None], (N_HEADS * dec, LANES))
    ck = cache_k_win[0].reshape(nseq, WINDOW, KV_WIDTH)
    cv = cache_v_win[0].reshape(nseq, WINDOW, KV_WIDTH)
    y_s, nk_s, nv_s, nc_s = _sample_mixer(p_s, xs, ck, cv, state_conv[0], stabs, sink_col, wdw, bdw, lng, lnb,
                                          w_out_b, gf)
    y_sample = y_s.reshape(nseq, dec, D_MODEL)

    kv_shape = (1, -1, WINDOW, N_KV_HEADS, HEAD_DIM)
    return (y_prompt, y_sample, nk_p.reshape(kv_shape), nv_p.reshape(kv_shape), nc_p[None],
            nk_s.reshape(kv_shape), nv_s.reshape(kv_shape), nc_s[None])
```

```python
import jax
import jax.numpy as jnp
from jax import lax
from jax.experimental import pallas as pl
from jax.experimental.pallas import tpu as pltpu

D_MODEL = 2048
N_META = 16
D_ATTN = 1024
D_CONV = 1024
HEAD_DIM = 64
N_HEADS = 16
N_KV_HEADS = 2
KV_WIDTH = N_KV_HEADS * HEAD_DIM
WINDOW = 128
BLOCK = 128
ROPE_DIM = 16
ROPE_THETA = 500000.0
CONV_WIDTH = 31
PREFIX = CONV_WIDTH - 1
D_PROJ = 2 * D_ATTN + 2 * KV_WIDTH + 3 * D_CONV
EPS = 1e-5
PAST_LEN = 8192

Q0 = 0
K0 = D_ATTN
V0 = K0 + KV_WIDTH
GA0 = V0 + KV_WIDTH
A0 = GA0 + D_ATTN
B0 = A0 + D_CONV
GC0 = B0 + D_CONV

LANES = 128
SUBLANES = 8
CHUNKS = D_ATTN // LANES
CHUNKS_PER_KV = CHUNKS // N_KV_HEADS
PAD = 32
NEG = -1e30

TQ = 256
VMEM_LIMIT = 56 * 1024 * 1024

f32 = jnp.float32
bf16 = jnp.bfloat16


def _rms_norm(x, g):
    return x * lax.rsqrt(jnp.mean(x * x, axis=-1, keepdims=True) + EPS) * g


NEG_LOG2_E = -1.4426950408889634


def _sigmoid(x):
    return 1.0 / (1.0 + jnp.exp2(x * NEG_LOG2_E))


def _silu(x):
    return x * _sigmoid(x)


def _rope(x, cos, sin_prev, sin_next):
    outs = []
    for c in range(x.shape[1] // LANES):
        xc = x[:, c * LANES:(c + 1) * LANES]
        outs.append(xc * cos + pltpu.roll(xc, 8, 1) * sin_prev + pltpu.roll(xc, LANES - 8, 1) * sin_next)
    return outs[0] if len(outs) == 1 else jnp.concatenate(outs, axis=1)


def _half_variants(x):
    lo = lax.broadcasted_iota(jnp.int32, x.shape, 1) < HEAD_DIM
    xr = pltpu.roll(x, HEAD_DIM, 1)
    zero = jnp.zeros_like(x)
    return (jnp.where(lo, x, zero).astype(bf16), jnp.where(lo, zero, xr).astype(bf16),
            jnp.where(lo, xr, zero).astype(bf16), jnp.where(lo, zero, x).astype(bf16))


def _value_variants_t(v):
    vt = v.T
    top, bot = vt[0:HEAD_DIM], vt[HEAD_DIM:2 * HEAD_DIM]
    ones = jnp.ones_like(top)
    return tuple(jnp.concatenate(pair, axis=0).astype(bf16)
                 for pair in ((top, ones), (ones, top), (bot, ones), (ones, bot)))


def _layer_norm(x, g, b):
    mu = jnp.mean(x, axis=-1, keepdims=True)
    xc = x - mu
    return xc * lax.rsqrt(jnp.mean(xc * xc, axis=-1, keepdims=True) + EPS) * g + b


def _rope_tables(pos):
    half = ROPE_DIM // 2
    inv = ROPE_THETA ** (-jnp.arange(0, ROPE_DIM, 2, dtype=f32) / ROPE_DIM)
    ang = pos.astype(f32)[:, None] * inv[None, :]
    cos, sin = jnp.cos(ang), jnp.sin(ang)
    t = pos.shape[0]
    ones = jnp.ones((t, HEAD_DIM - ROPE_DIM), f32)
    zeros = jnp.zeros((t, HEAD_DIM - ROPE_DIM), f32)
    zh = jnp.zeros((t, half), f32)
    cos64 = jnp.concatenate([cos, cos, ones], axis=1)
    prev64 = jnp.concatenate([zh, sin, zeros], axis=1)
    next64 = jnp.concatenate([-sin, zh, zeros], axis=1)
    rep = LANES // HEAD_DIM
    return jnp.tile(cos64, (1, rep)), jnp.tile(prev64, (1, rep)), jnp.tile(next64, (1, rep))


PROJ_BN = 768


def _proj_kernel(x_ref, meta_ref, g_ref, w_ref, p_ref, pm_ref, wb_ref, u_ref):
    n = x_ref.shape[0]

    @pl.when(pl.program_id(0) == 0)
    def _():
        u_ref[0:n, :] = _rms_norm(x_ref[...], g_ref[...]).astype(bf16)
        u_ref[n:n + N_META, :] = _rms_norm(meta_ref[...], g_ref[...]).astype(bf16)

    w = w_ref[...].astype(bf16)
    wb_ref[...] = w
    p = jnp.dot(u_ref[...], w, preferred_element_type=f32)
    p_ref[...] = p[0:n]
    pm_ref[...] = p[n:n + N_META]


def _project_sample(x, meta, g, w_in):
    n = x.shape[0]
    once = pl.Buffered(1)
    return pl.pallas_call(
        _proj_kernel,
        grid=(D_PROJ // PROJ_BN,),
        in_specs=[
            pl.BlockSpec((n, D_MODEL), lambda j: (0, 0), pipeline_mode=once),
            pl.BlockSpec((N_META, D_MODEL), lambda j: (0, 0), pipeline_mode=once),
            pl.BlockSpec((1, D_MODEL), lambda j: (0, 0), pipeline_mode=once),
            pl.BlockSpec((D_MODEL, PROJ_BN), lambda j: (0, j)),
        ],
        out_specs=[
            pl.BlockSpec((n, PROJ_BN), lambda j: (0, j)),
            pl.BlockSpec((N_META, PROJ_BN), lambda j: (0, j)),
            pl.BlockSpec((D_MODEL, PROJ_BN), lambda j: (0, j)),
        ],
        out_shape=[jax.ShapeDtypeStruct((n, D_PROJ), f32), jax.ShapeDtypeStruct((N_META, D_PROJ), f32),
                   jax.ShapeDtypeStruct((D_MODEL, D_PROJ), bf16)],
        scratch_shapes=[pltpu.VMEM((n + N_META, D_MODEL), bf16)],
        compiler_params=pltpu.CompilerParams(dimension_semantics=("arbitrary",), vmem_limit_bytes=VMEM_LIMIT),
    )(x, meta, g, w_in)


CONV_OFF = PAD - PREFIX
CONV_TAP_TILES = (CONV_OFF + CONV_WIDTH - 1) // SUBLANES + 1


CHAIN_ROLL_EVERY = 8


def _chain_link(chain, y):
    if chain[0] is not None:
        y = y + chain[0]
    token = jnp.minimum(jnp.abs(y), 0.0)
    chain[1] += 1
    if chain[1] % CHAIN_ROLL_EVERY == 0:
        token = pltpu.roll(token, 1, 1)
    chain[0] = token
    return y


def _conv_chunk(load_tile, n_seq, n_tiles, wdw_ref, c, chain=None):
    row = lax.broadcasted_iota(jnp.int32, (SUBLANES, LANES), 0)
    lanes = slice(c * LANES, (c + 1) * LANES)
    w = [jnp.broadcast_to(wdw_ref[k:k + 1, lanes], (SUBLANES, LANES)) for k in range(CONV_WIDTH)]
    outs = []
    for s in range(n_seq):
        g = [load_tile(s, j, lanes) for j in range(n_tiles + CONV_TAP_TILES - 1)]

        def z(r, i):
            acc = None
            for a in range(CONV_TAP_TILES):
                k = SUBLANES * a + r - CONV_OFF
                if 0 <= k < CONV_WIDTH:
                    term = w[k] * g[i + a]
                    acc = term if acc is None else acc + term
            return acc

        prev = [z(r, 0) for r in range(1, SUBLANES)]
        for i in range(n_tiles):
            nxt = [z(r, i + 1) for r in range(1, SUBLANES)]
            y = z(0, i)
            for r in range(1, SUBLANES):
                y = y + pltpu.roll(jnp.where(row >= r, prev[r - 1], nxt[r - 1]), SUBLANES - r, 0)
            outs.append(y if chain is None else _chain_link(chain, y))
            prev = nxt
    return jnp.concatenate(outs, axis=0)


def _causal_conv(load_tile, n_seq, n_tiles, wdw_ref):
    return jnp.concatenate([_conv_chunk(load_tile, n_seq, n_tiles, wdw_ref, c)
                            for c in range(D_CONV // LANES)], axis=1)


def _conv_ln(y, bdw_ref, lng_ref, lnb_ref):
    return _silu(_layer_norm(y + bdw_ref[...], lng_ref[...], lnb_ref[...]))


def _band_bias_t(first_valid_key):
    key = lax.broadcasted_iota(jnp.int32, (2 * BLOCK, BLOCK), 0)
    qry = lax.broadcasted_iota(jnp.int32, (2 * BLOCK, BLOCK), 1)
    valid = (key >= qry) & (key <= qry + WINDOW) & (key >= first_valid_key)
    return jnp.where(valid, 0.0, NEG).astype(f32)


def _attn_piece(q_blk, kvar_ref, vt_ref, row0, bias_t, sinks_ref, kv, par):
    qs = jnp.concatenate([q_blk[:, i * LANES:(i + 1) * LANES] for i in range(CHUNKS_PER_KV)], axis=0)
    kk = kvar_ref[kv * 2 + par, row0:row0 + 2 * BLOCK, :]
    vt = vt_ref[kv * 2 + par, :, row0:row0 + 2 * BLOCK]
    st = lax.dot_general(kk, qs, (((1,), (1,)), ((), ())), preferred_element_type=f32)
    es, sink_terms = [], []
    for i in range(CHUNKS_PER_KV):
        blk = st[:, i * BLOCK:(i + 1) * BLOCK] + bias_t
        sink = sinks_ref[kv * 8 + 2 * i + par]
        m = jnp.maximum(jnp.max(blk, axis=0, keepdims=True), sink)
        es.append(jnp.exp(blk - m).astype(bf16))
        sink_terms.append(jnp.exp(sink - m))
    out_t = jnp.dot(vt, jnp.concatenate(es, axis=1), preferred_element_type=f32)
    vals, sums = (0, HEAD_DIM) if par == 0 else (HEAD_DIM, 0)
    den = out_t[sums:sums + 1, :] + jnp.concatenate(sink_terms, axis=1)
    return out_t[vals:vals + HEAD_DIM, :] * (1.0 / den)


PROJ_PIECE = 512
OUT_PIECE = 512
CONV_PIECE = 256


def _prompt_kernel(sinks_ref, x_ref, tab_ref, pm_ref, mcos_ref, msp_ref, msn_ref,
                   g_ref, win_ref, wdw_ref, bdw_ref, lng_ref, lnb_ref, wout_ref, gf_ref,
                   y_ref, nk_ref, nv_ref, nc_ref, kvar_ref, vt_ref, glu_ref, u_ref, z_ref):
    t = pl.program_id(1)
    nt = pl.num_programs(1)

    @pl.when(t == 0)
    def _():
        pm = pm_ref[...]
        km = _rope(pm[:, K0:K0 + KV_WIDTH], mcos_ref[...], msp_ref[...], msn_ref[...])
        vm = pm[:, V0:V0 + KV_WIDTH]
        invalid = jnp.zeros((BLOCK - N_META, LANES), f32)
        kvar_ref[:, 0:BLOCK - N_META, :] = jnp.zeros((4, BLOCK - N_META, LANES), bf16)
        for i, var in enumerate(_half_variants(km)):
            kvar_ref[i, BLOCK - N_META:BLOCK, :] = var
        for i, var in enumerate(_value_variants_t(jnp.concatenate([invalid, vm], axis=0))):
            vt_ref[i, :, 0:BLOCK] = var
        glu_ref[0:PAD - N_META, :] = jnp.zeros((PAD - N_META, D_CONV), f32)
        glu_ref[PAD - N_META:PAD, :] = pm[:, A0:A0 + D_CONV] * _sigmoid(pm[:, B0:B0 + D_CONV])

    u_ref[...] = _rms_norm(x_ref[0], g_ref[...]).astype(bf16)

    def proj(c0, width):
        return jnp.dot(u_ref[...], win_ref[:, c0:c0 + width], preferred_element_type=f32)

    cos, sp, sn, cos_q, sp_q, sn_q = (tab_ref[:, i * LANES:(i + 1) * LANES] for i in range(6))

    kv = proj(K0, 2 * KV_WIDTH)
    k = _rope(kv[:, 0:KV_WIDTH], cos, sp, sn)
    v = kv[:, KV_WIDTH:2 * KV_WIDTH]
    for i, var in enumerate(_half_variants(k)):
        kvar_ref[i, BLOCK:BLOCK + TQ, :] = var
    for j in range(TQ // BLOCK):
        for i, var in enumerate(_value_variants_t(v[j * BLOCK:(j + 1) * BLOCK])):
            vt_ref[i, :, (j + 1) * BLOCK:(j + 2) * BLOCK] = var

    def load_tile(s, i, lanes):
        return glu_ref[i * SUBLANES:(i + 1) * SUBLANES, lanes]

    def glu(c0, a, b):
        glu_ref[PAD:PAD + TQ, c0:c0 + PROJ_PIECE] = a * _sigmoid(b)

    conv_chain = [None, 0]

    def conv_piece(p):
        first = p * CONV_PIECE // LANES
        return jnp.concatenate([_conv_chunk(load_tile, 1, TQ // SUBLANES, wdw_ref, c, conv_chain)
                                for c in range(first, first + CONV_PIECE // LANES)], axis=1)

    def out_part(k0, width):
        lhs = z_ref[:, k0:k0 + width]
        return [jnp.dot(lhs, wout_ref[k0:k0 + width, c:c + OUT_PIECE], preferred_element_type=f32)
                for c in range(0, D_MODEL, OUT_PIECE)]

    def attention(kvh, q_raw, ga_raw):
        qh = _rope(q_raw, cos_q, sp_q, sn_q).astype(bf16)
        blocks = []
        for j in range(TQ // BLOCK):
            first_valid = jnp.where(t == 0, BLOCK - N_META, 0) if j == 0 else 0
            bias_t = _band_bias_t(first_valid)
            qb = qh[j * BLOCK:(j + 1) * BLOCK]
            acc = jnp.concatenate([_attn_piece(qb, kvar_ref, vt_ref, j * BLOCK, bias_t, sinks_ref, kvh, par)
                                   for par in range(2)], axis=0)
            blocks.append(jnp.concatenate([acc[:, i * BLOCK:(i + 1) * BLOCK].T for i in range(CHUNKS_PER_KV)], axis=1))
        z_attn = jnp.concatenate(blocks, axis=0) * _silu(ga_raw)
        z_ref[:, kvh * group:(kvh + 1) * group] = z_attn.astype(bf16)

    group = CHUNKS_PER_KV * LANES
    assert D_CONV == 2 * PROJ_PIECE and group == PROJ_PIECE and PROJ_PIECE == 2 * CONV_PIECE

    glu(0, proj(A0, PROJ_PIECE), proj(B0, PROJ_PIECE))
    a_hi = proj(A0 + PROJ_PIECE, PROJ_PIECE)
    conv = [conv_piece(0)]
    b_hi = proj(B0 + PROJ_PIECE, PROJ_PIECE)
    conv.append(conv_piece(1))
    glu(PROJ_PIECE, a_hi, b_hi)
    gc_lo = proj(GC0, PROJ_PIECE)
    conv.append(conv_piece(2))
    gc_hi = proj(GC0 + PROJ_PIECE, PROJ_PIECE)
    conv.append(conv_piece(3))
    q_lo = proj(Q0, group)
    z_conv = (_conv_ln(jnp.concatenate(conv, axis=1), bdw_ref, lng_ref, lnb_ref)
              * _silu(jnp.concatenate([gc_lo, gc_hi], axis=1)))
    z_ref[:, D_ATTN:D_ATTN + D_CONV] = z_conv.astype(bf16)

    q_hi = proj(Q0 + group, group)
    ga_lo = proj(GA0, group)
    attention(0, q_lo, ga_lo)
    ga_hi = proj(GA0 + group, group)
    o_conv = out_part(D_ATTN, D_CONV)
    attention(1, q_hi, ga_hi)
    o_lo = out_part(0, group)
    o_hi = out_part(group, group)

    hs, ssq = [], None
    for i, c in enumerate(range(0, D_MODEL, OUT_PIECE)):
        h = x_ref[0, :, c:c + OUT_PIECE] + (o_conv[i] + o_lo[i] + o_hi[i])
        part = jnp.sum(h * h, axis=-1, keepdims=True)
        ssq = part if ssq is None else ssq + part
        hs.append(h)
    scale = lax.rsqrt(ssq * (1.0 / D_MODEL) + EPS)
    for i, h in enumerate(hs):
        y_ref[0, :, i * OUT_PIECE:(i + 1) * OUT_PIECE] = h * scale * gf_ref[:, i * OUT_PIECE:(i + 1) * OUT_PIECE]

    @pl.when(t == nt - 1)
    def _():
        nk_ref[0] = k[TQ - WINDOW:TQ]
        nv_ref[0] = v[TQ - WINDOW:TQ]
        nc_ref[0] = glu_ref[PAD + TQ - PREFIX:PAD + TQ, :]

    kvar_ref[:, 0:BLOCK, :] = kvar_ref[:, TQ:TQ + BLOCK, :]
    vt_ref[:, :, 0:BLOCK] = vt_ref[:, :, TQ:TQ + BLOCK]
    glu_ref[0:PAD, :] = glu_ref[TQ:TQ + PAD, :]


def _prompt_call(sinks, x, tabs, p_meta, mtabs, g, w_in, w_dw, b_dw, ln_g, ln_b, w_out, gf):
    nb, seq, _ = x.shape
    nt = seq // TQ
    once = pl.Buffered(1)

    def const(shape):
        return pl.BlockSpec(shape, lambda b, t: (0,) * len(shape), pipeline_mode=once)

    q_scale = HEAD_DIM ** -0.5
    tab = jnp.concatenate(list(tabs) + [tb * q_scale for tb in tabs], axis=1)
    tab_spec = pl.BlockSpec((TQ, 6 * LANES), lambda b, t: (t, 0))
    return pl.pallas_call(
        _prompt_kernel,
        grid=(nb, nt),
        in_specs=[
            pl.BlockSpec(memory_space=pltpu.SMEM),
            pl.BlockSpec((1, TQ, D_MODEL), lambda b, t: (b, t, 0)),
            tab_spec,
            const((N_META, D_PROJ)), const((N_META, LANES)), const((N_META, LANES)), const((N_META, LANES)),
            const((1, D_MODEL)), const((D_MODEL, D_PROJ)),
            const((CONV_WIDTH, D_CONV)), const((1, D_CONV)), const((1, D_CONV)), const((1, D_CONV)),
            const((D_ATTN + D_CONV, D_MODEL)), const((1, D_MODEL)),
        ],
        out_specs=[
            pl.BlockSpec((1, TQ, D_MODEL), lambda b, t: (b, t, 0)),
            pl.BlockSpec((1, WINDOW, KV_WIDTH), lambda b, t: (b, 0, 0)),
            pl.BlockSpec((1, WINDOW, KV_WIDTH), lambda b, t: (b, 0, 0)),
            pl.BlockSpec((1, PREFIX, D_CONV), lambda b, t: (b, 0, 0)),
        ],
        out_shape=[
            jax.ShapeDtypeStruct((nb, seq, D_MODEL), f32),
            jax.ShapeDtypeStruct((nb, WINDOW, KV_WIDTH), f32),
            jax.ShapeDtypeStruct((nb, WINDOW, KV_WIDTH), f32),
            jax.ShapeDtypeStruct((nb, PREFIX, D_CONV), f32),
        ],
        scratch_shapes=[
            pltpu.VMEM((4, BLOCK + TQ, LANES), bf16),
            pltpu.VMEM((4, LANES, BLOCK + TQ), bf16),
            pltpu.VMEM((PAD + TQ, D_CONV), f32),
            pltpu.VMEM((TQ, D_MODEL), bf16),
            pltpu.VMEM((TQ, D_ATTN + D_CONV), bf16),
        ],
        compiler_params=pltpu.CompilerParams(
            dimension_semantics=("arbitrary", "arbitrary"), vmem_limit_bytes=VMEM_LIMIT),
    )(sinks, x, tab, p_meta, *mtabs, g, w_in, w_dw, b_dw, ln_g, ln_b, w_out, gf)


SEQ_BLOCK = 16
SEQ_UNROLL = 4
DEC = 8
KEYS = 2 * WINDOW


def _sample_mixer_kernel(p_ref, x_ref, ck_ref, cv_ref, st_ref, cos_ref, sp_ref, sn_ref, sink_ref,
                         wdw_ref, bdw_ref, lng_ref, lnb_ref, wout_ref, gf_ref,
                         y_ref, nk_ref, nv_ref, nc_ref, q_scr, k_scr, v_scr, attn_scr, cbuf):
    rows = SEQ_BLOCK * DEC
    cos, sp, sn = cos_ref[...], sp_ref[...], sn_ref[...]

    k = _rope(p_ref[:, K0:K0 + KV_WIDTH], cos, sp, sn)
    v = p_ref[:, V0:V0 + KV_WIDTH]
    q_scr[...] = _rope(p_ref[:, Q0:Q0 + D_ATTN], cos, sp, sn) * (HEAD_DIM ** -0.5)
    k_scr[...] = k
    v_scr[...] = v

    nk_ref[:, 0:WINDOW - DEC, :] = ck_ref[:, DEC:WINDOW, :]
    nk_ref[:, WINDOW - DEC:WINDOW, :] = k.reshape(SEQ_BLOCK, DEC, KV_WIDTH)
    nv_ref[:, 0:WINDOW - DEC, :] = cv_ref[:, DEC:WINDOW, :]
    nv_ref[:, WINDOW - DEC:WINDOW, :] = v.reshape(SEQ_BLOCK, DEC, KV_WIDTH)

    glu = p_ref[:, A0:A0 + D_CONV] * _sigmoid(p_ref[:, B0:B0 + D_CONV])
    cbuf[:, 0:SUBLANES, :] = jnp.zeros((SEQ_BLOCK, SUBLANES, D_CONV), f32)
    cbuf[:, CONV_OFF:PAD, :] = st_ref[...]
    cbuf[:, PAD:PAD + DEC, :] = glu.reshape(SEQ_BLOCK, DEC, D_CONV)
    nc_ref[...] = cbuf[:, PAD + DEC - PREFIX:PAD + DEC, :]
    conv = _causal_conv(lambda s, j, lanes: cbuf[s, j * SUBLANES:(j + 1) * SUBLANES, lanes],
                        SEQ_BLOCK, DEC // SUBLANES, wdw_ref)
    z_conv = _conv_ln(conv, bdw_ref, lng_ref, lnb_ref) * _silu(p_ref[:, GC0:GC0 + D_CONV])
    o_conv = jnp.dot(z_conv.astype(bf16), wout_ref[D_ATTN:D_ATTN + D_CONV, :], preferred_element_type=f32)

    row = lax.broadcasted_iota(jnp.int32, (N_HEADS * DEC, KEYS), 0) % DEC
    col = lax.broadcasted_iota(jnp.int32, (N_HEADS * DEC, KEYS), 1)
    valid = ((col < WINDOW) & (col >= row)) | ((col >= WINDOW) & (col - WINDOW <= row))
    bias = jnp.where(valid, 0.0, NEG).astype(f32)
    lo = lax.broadcasted_iota(jnp.int32, (DEC, LANES), 1) < HEAD_DIM
    zero = jnp.zeros((DEC, LANES), f32)
    pad_rows = jnp.zeros((KEYS - WINDOW - DEC, LANES), f32)
    sink = sink_ref[...]

    def body(s, carry):
        r0 = pl.multiple_of(s * DEC, DEC)
        qs = q_scr[pl.ds(r0, DEC), :]
        pieces = [None] * N_HEADS
        for c in range(CHUNKS):
            qc = qs[:, c * LANES:(c + 1) * LANES]
            qr = pltpu.roll(qc, HEAD_DIM, 1)
            if c < CHUNKS_PER_KV:
                pieces[2 * c], pieces[2 * c + 1] = jnp.where(lo, qc, zero), jnp.where(lo, qr, zero)
            else:
                pieces[2 * c], pieces[2 * c + 1] = jnp.where(lo, zero, qr), jnp.where(lo, zero, qc)
        qrows = jnp.concatenate(pieces, axis=0).astype(bf16)
        kf = jnp.concatenate([ck_ref[s], k_scr[pl.ds(r0, DEC), :], pad_rows], axis=0).astype(bf16)
        vf = jnp.concatenate([cv_ref[s], v_scr[pl.ds(r0, DEC), :], pad_rows], axis=0).astype(bf16)
        sc = lax.dot_general(qrows, kf, (((1,), (1,)), ((), ())), preferred_element_type=f32) + bias
        m = jnp.maximum(jnp.broadcast_to(jnp.max(sc, axis=1, keepdims=True), sink.shape), sink)
        e = jnp.exp(sc - jnp.concatenate([m] * (KEYS // LANES), axis=1))
        den = jnp.broadcast_to(jnp.sum(e, axis=1, keepdims=True), sink.shape) + jnp.exp(sink - m)
        o = jnp.dot(e.astype(bf16), vf, preferred_element_type=f32) * (1.0 / den)
        outs = []
        for c in range(CHUNKS):
            ev, od = o[2 * c * DEC:(2 * c + 1) * DEC], o[(2 * c + 1) * DEC:(2 * c + 2) * DEC]
            if c < CHUNKS_PER_KV:
                outs.append(jnp.where(lo, ev, pltpu.roll(od, HEAD_DIM, 1)))
            else:
                outs.append(jnp.where(lo, pltpu.roll(ev, HEAD_DIM, 1), od))
        attn_scr[pl.ds(r0, DEC), :] = jnp.concatenate(outs, axis=1)
        return carry

    lax.fori_loop(0, SEQ_BLOCK, body, 0, unroll=SEQ_UNROLL)

    z_attn = attn_scr[...] * _silu(p_ref[:, GA0:GA0 + D_ATTN])
    o_attn = jnp.dot(z_attn.astype(bf16), wout_ref[0:D_ATTN, :], preferred_element_type=f32)
    y_ref[...] = _rms_norm(x_ref[...] + (o_conv + o_attn), gf_ref[...])


def _sample_mixer(p, x, ck, cv, st, tabs, sink_col, w_dw, b_dw, ln_g, ln_b, w_out, gf):
    nseq = ck.shape[0]
    rows = SEQ_BLOCK * DEC

    def const(shape):
        return pl.BlockSpec(shape, lambda i: (0,) * len(shape))

    return pl.pallas_call(
        _sample_mixer_kernel,
        grid=(nseq // SEQ_BLOCK,),
        in_specs=[
            pl.BlockSpec((rows, D_PROJ), lambda i: (i, 0)),
            pl.BlockSpec((rows, D_MODEL), lambda i: (i, 0)),
            pl.BlockSpec((SEQ_BLOCK, WINDOW, KV_WIDTH), lambda i: (i, 0, 0)),
            pl.BlockSpec((SEQ_BLOCK, WINDOW, KV_WIDTH), lambda i: (i, 0, 0)),
            pl.BlockSpec((SEQ_BLOCK, PREFIX, D_CONV), lambda i: (i, 0, 0)),
            const((rows, LANES)), const((rows, LANES)), const((rows, LANES)),
            const((N_HEADS * DEC, LANES)),
            const((CONV_WIDTH, D_CONV)), const((1, D_CONV)), const((1, D_CONV)), const((1, D_CONV)),
            pl.BlockSpec((D_ATTN + D_CONV, D_MODEL), lambda i: (0, 0), pipeline_mode=pl.Buffered(1)),
            const((1, D_MODEL)),
        ],
        out_specs=[
            pl.BlockSpec((rows, D_MODEL), lambda i: (i, 0)),
            pl.BlockSpec((SEQ_BLOCK, WINDOW, KV_WIDTH), lambda i: (i, 0, 0)),
            pl.BlockSpec((SEQ_BLOCK, WINDOW, KV_WIDTH), lambda i: (i, 0, 0)),
            pl.BlockSpec((SEQ_BLOCK, PREFIX, D_CONV), lambda i: (i, 0, 0)),
        ],
        out_shape=[
            jax.ShapeDtypeStruct((nseq * DEC, D_MODEL), f32),
            jax.ShapeDtypeStruct((nseq, WINDOW, KV_WIDTH), f32),
            jax.ShapeDtypeStruct((nseq, WINDOW, KV_WIDTH), f32),
            jax.ShapeDtypeStruct((nseq, PREFIX, D_CONV), f32),
        ],
        scratch_shapes=[
            pltpu.VMEM((rows, D_ATTN), f32),
            pltpu.VMEM((rows, KV_WIDTH), f32),
            pltpu.VMEM((rows, KV_WIDTH), f32),
            pltpu.VMEM((rows, D_ATTN), f32),
            pltpu.VMEM((SEQ_BLOCK, PAD + DEC, D_CONV), f32),
        ],
        compiler_params=pltpu.CompilerParams(dimension_semantics=("arbitrary",), vmem_limit_bytes=VMEM_LIMIT),
    )(p, x, ck, cv, st, *tabs, sink_col, w_dw, b_dw, ln_g, ln_b, w_out, gf)


def kernel(x_prompt, x_sample, cache_k_win, cache_v_win, state_conv, meta_tokens, norm_gain, w_in,
           attn_sinks, w_dw, b_dw, conv_norm_gain, conv_norm_bias, w_out, final_norm_gain):
    nb, seq, _ = x_prompt.shape
    nseq, dec, _ = x_sample.shape
    assert norm_gain.shape[0] == 1 and dec == DEC and seq % TQ == 0 and nseq % SEQ_BLOCK == 0

    g = norm_gain[0][None]
    w_out_b = w_out[0].astype(bf16)
    sinks = attn_sinks[0]
    wdw, bdw = w_dw[0], b_dw[0][None]
    lng, lnb = conv_norm_gain[0][None], conv_norm_bias[0][None]
    gf = final_norm_gain[None]

    xs = x_sample.reshape(nseq * dec, D_MODEL)
    p_s, p_meta, w_in_b = _project_sample(xs, meta_tokens, g, w_in[0])

    tabs = _rope_tables(N_META + jnp.arange(seq))
    mtabs = _rope_tables(jnp.arange(N_META))
    y_prompt, nk_p, nv_p, nc_p = _prompt_call(sinks, x_prompt, tabs, p_meta, mtabs, g, w_in_b,
                                               wdw, bdw, lng, lnb, w_out_b, gf)

    stabs = _rope_tables(PAST_LEN + jnp.arange(SEQ_BLOCK * dec) % dec)
    sink_col = jnp.broadcast_to(jnp.repeat(sinks, dec)[:, None], (N_HEADS * dec, LANES))
    ck = cache_k_win[0].reshape(nseq, WINDOW, KV_WIDTH)
    cv = cache_v_win[0].reshape(nseq, WINDOW, KV_WIDTH)
    y_s, nk_s, nv_s, nc_s = _sample_mixer(p_s, xs, ck, cv, state_conv[0], stabs, sink_col, wdw, bdw, lng, lnb,
                                          w_out_b, gf)
    y_sample = y_s.reshape(nseq, dec, D_MODEL)

    kv_shape = (1, -1, WINDOW, N_KV_HEADS, HEAD_DIM)
    return (y_prompt, y_sample, nk_p.reshape(kv_shape), nv_p.reshape(kv_shape), nc_p[None],
            nk_s.reshape(kv_shape), nv_s.reshape(kv_shape), nc_s[None])
```

```python
import jax
import jax.numpy as jnp
from jax import lax
from jax.experimental import pallas as pl
from jax.experimental.pallas import tpu as pltpu

D_MODEL = 2048
N_META = 16
D_ATTN = 1024
D_CONV = 1024
HEAD_DIM = 64
N_HEADS = 16
N_KV_HEADS = 2
KV_WIDTH = N_KV_HEADS * HEAD_DIM
WINDOW = 128
BLOCK = 128
ROPE_DIM = 16
ROPE_THETA = 500000.0
CONV_WIDTH = 31
PREFIX = CONV_WIDTH - 1
D_PROJ = 2 * D_ATTN + 2 * KV_WIDTH + 3 * D_CONV
EPS = 1e-5
PAST_LEN = 8192

Q0 = 0
K0 = D_ATTN
V0 = K0 + KV_WIDTH
GA0 = V0 + KV_WIDTH
A0 = GA0 + D_ATTN
B0 = A0 + D_CONV
GC0 = B0 + D_CONV

LANES = 128
SUBLANES = 8
CHUNKS = D_ATTN // LANES
CHUNKS_PER_KV = CHUNKS // N_KV_HEADS
PAD = 32
NEG = -1e30

TQ = 256
VMEM_LIMIT = 56 * 1024 * 1024

f32 = jnp.float32
bf16 = jnp.bfloat16


def _rms_norm(x, g):
    return x * lax.rsqrt(jnp.mean(x * x, axis=-1, keepdims=True) + EPS) * g


NEG_LOG2_E = -1.4426950408889634


def _sigmoid(x):
    return 1.0 / (1.0 + jnp.exp2(x * NEG_LOG2_E))


def _silu(x):
    return x * _sigmoid(x)


def _rope(x, cos, sin_prev, sin_next):
    outs = []
    for c in range(x.shape[1] // LANES):
        xc = x[:, c * LANES:(c + 1) * LANES]
        outs.append(xc * cos + pltpu.roll(xc, 8, 1) * sin_prev + pltpu.roll(xc, LANES - 8, 1) * sin_next)
    return outs[0] if len(outs) == 1 else jnp.concatenate(outs, axis=1)


def _half_variants(x):
    lo = lax.broadcasted_iota(jnp.int32, x.shape, 1) < HEAD_DIM
    xr = pltpu.roll(x, HEAD_DIM, 1)
    zero = jnp.zeros_like(x)
    return (jnp.where(lo, x, zero).astype(bf16), jnp.where(lo, zero, xr).astype(bf16),
            jnp.where(lo, xr, zero).astype(bf16), jnp.where(lo, zero, x).astype(bf16))


def _value_variants_t(v):
    vt = v.T
    top, bot = vt[0:HEAD_DIM], vt[HEAD_DIM:2 * HEAD_DIM]
    ones = jnp.ones_like(top)
    return tuple(jnp.concatenate(pair, axis=0).astype(bf16)
                 for pair in ((top, ones), (ones, top), (bot, ones), (ones, bot)))


def _layer_norm(x, g, b):
    mu = jnp.mean(x, axis=-1, keepdims=True)
    xc = x - mu
    return xc * lax.rsqrt(jnp.mean(xc * xc, axis=-1, keepdims=True) + EPS) * g + b


def _rope_tables(pos):
    half = ROPE_DIM // 2
    inv = ROPE_THETA ** (-jnp.arange(0, ROPE_DIM, 2, dtype=f32) / ROPE_DIM)
    ang = pos.astype(f32)[:, None] * inv[None, :]
    cos, sin = jnp.cos(ang), jnp.sin(ang)
    t = pos.shape[0]
    ones = jnp.ones((t, HEAD_DIM - ROPE_DIM), f32)
    zeros = jnp.zeros((t, HEAD_DIM - ROPE_DIM), f32)
    zh = jnp.zeros((t, half), f32)
    cos64 = jnp.concatenate([cos, cos, ones], axis=1)
    prev64 = jnp.concatenate([zh, sin, zeros], axis=1)
    next64 = jnp.concatenate([-sin, zh, zeros], axis=1)
    rep = LANES // HEAD_DIM
    return jnp.tile(cos64, (1, rep)), jnp.tile(prev64, (1, rep)), jnp.tile(next64, (1, rep))


PROJ_BN = 768


def _proj_kernel(x_ref, meta_ref, g_ref, w_ref, p_ref, pm_ref, wb_ref, u_ref):
    n = x_ref.shape[0]

    @pl.when(pl.program_id(0) == 0)
    def _():
        u_ref[0:n, :] = _rms_norm(x_ref[...], g_ref[...]).astype(bf16)
        u_ref[n:n + N_META, :] = _rms_norm(meta_ref[...], g_ref[...]).astype(bf16)

    w = w_ref[...].astype(bf16)
    wb_ref[...] = w
    p = jnp.dot(u_ref[...], w, preferred_element_type=f32)
    p_ref[...] = p[0:n]
    pm_ref[...] = p[n:n + N_META]


def _project_sample(x, meta, g, w_in):
    n = x.shape[0]
    once = pl.Buffered(1)
    return pl.pallas_call(
        _proj_kernel,
        grid=(D_PROJ // PROJ_BN,),
        in_specs=[
            pl.BlockSpec((n, D_MODEL), lambda j: (0, 0), pipeline_mode=once),
            pl.BlockSpec((N_META, D_MODEL), lambda j: (0, 0), pipeline_mode=once),
            pl.BlockSpec((1, D_MODEL), lambda j: (0, 0), pipeline_mode=once),
            pl.BlockSpec((D_MODEL, PROJ_BN), lambda j: (0, j)),
        ],
        out_specs=[
            pl.BlockSpec((n, PROJ_BN), lambda j: (0, j)),
            pl.BlockSpec((N_META, PROJ_BN), lambda j: (0, j)),
            pl.BlockSpec((D_MODEL, PROJ_BN), lambda j: (0, j)),
        ],
        out_shape=[jax.ShapeDtypeStruct((n, D_PROJ), f32), jax.ShapeDtypeStruct((N_META, D_PROJ), f32),
                   jax.ShapeDtypeStruct((D_MODEL, D_PROJ), bf16)],
        scratch_shapes=[pltpu.VMEM((n + N_META, D_MODEL), bf16)],
        compiler_params=pltpu.CompilerParams(dimension_semantics=("arbitrary",), vmem_limit_bytes=VMEM_LIMIT),
    )(x, meta, g, w_in)


CONV_OFF = PAD - PREFIX
CONV_TAP_TILES = (CONV_OFF + CONV_WIDTH - 1) // SUBLANES + 1


CHAIN_ROLL_EVERY = 8


def _chain_link(chain, y):
    if chain[0] is not None:
        y = y + chain[0]
    token = jnp.minimum(jnp.abs(y), 0.0)
    chain[1] += 1
    if chain[1] % CHAIN_ROLL_EVERY == 0:
        token = pltpu.roll(token, 1, 1)
    chain[0] = token
    return y


def _conv_chunk(load_tile, n_seq, n_tiles, wdw_ref, c, chain=None):
    row = lax.broadcasted_iota(jnp.int32, (SUBLANES, LANES), 0)
    lanes = slice(c * LANES, (c + 1) * LANES)
    w = [jnp.broadcast_to(wdw_ref[k:k + 1, lanes], (SUBLANES, LANES)) for k in range(CONV_WIDTH)]
    outs = []
    for s in range(n_seq):
        g = [load_tile(s, j, lanes) for j in range(n_tiles + CONV_TAP_TILES - 1)]

        def z(r, i):
            acc = None
            for a in range(CONV_TAP_TILES):
                k = SUBLANES * a + r - CONV_OFF
                if 0 <= k < CONV_WIDTH:
                    term = w[k] * g[i + a]
                    acc = term if acc is None else acc + term
            return acc

        prev = [z(r, 0) for r in range(1, SUBLANES)]
        for i in range(n_tiles):
            nxt = [z(r, i + 1) for r in range(1, SUBLANES)]
            y = z(0, i)
            for r in range(1, SUBLANES):
                y = y + pltpu.roll(jnp.where(row >= r, prev[r - 1], nxt[r - 1]), SUBLANES - r, 0)
            outs.append(y if chain is None else _chain_link(chain, y))
            prev = nxt
    return jnp.concatenate(outs, axis=0)


def _causal_conv(load_tile, n_seq, n_tiles, wdw_ref):
    return jnp.concatenate([_conv_chunk(load_tile, n_seq, n_tiles, wdw_ref, c)
                            for c in range(D_CONV // LANES)], axis=1)


def _conv_ln(y, bdw_ref, lng_ref, lnb_ref):
    return _silu(_layer_norm(y + bdw_ref[...], lng_ref[...], lnb_ref[...]))


def _band_bias_t(first_valid_key):
    key = lax.broadcasted_iota(jnp.int32, (2 * BLOCK, BLOCK), 0)
    qry = lax.broadcasted_iota(jnp.int32, (2 * BLOCK, BLOCK), 1)
    valid = (key >= qry) & (key <= qry + WINDOW) & (key >= first_valid_key)
    return jnp.where(valid, 0.0, NEG).astype(f32)


def _attn_piece(q_blk, kvar_ref, vt_ref, row0, bias_t, sinks_ref, kv, par):
    qs = jnp.concatenate([q_blk[:, i * LANES:(i + 1) * LANES] for i in range(CHUNKS_PER_KV)], axis=0)
    kk = kvar_ref[kv * 2 + par, row0:row0 + 2 * BLOCK, :]
    vt = vt_ref[kv * 2 + par, :, row0:row0 + 2 * BLOCK]
    st = lax.dot_general(kk, qs, (((1,), (1,)), ((), ())), preferred_element_type=f32)
    es, sink_terms = [], []
    for i in range(CHUNKS_PER_KV):
        blk = st[:, i * BLOCK:(i + 1) * BLOCK] + bias_t
        sink = sinks_ref[kv * 8 + 2 * i + par]
        m = jnp.maximum(jnp.max(blk, axis=0, keepdims=True), sink)
        es.append(jnp.exp(blk - m).astype(bf16))
        sink_terms.append(jnp.exp(sink - m))
    out_t = jnp.dot(vt, jnp.concatenate(es, axis=1), preferred_element_type=f32)
    vals, sums = (0, HEAD_DIM) if par == 0 else (HEAD_DIM, 0)
    den = out_t[sums:sums + 1, :] + jnp.concatenate(sink_terms, axis=1)
    return out_t[vals:vals + HEAD_DIM, :] * (1.0 / den)


PROJ_PIECE = 512
OUT_PIECE = 512
CONV_PIECE = 256


class _Row:
    def __init__(self, ref, row, c0=0, width=D_MODEL):
        self.ref, self.row, self.c0, self.width = ref, row, c0, width

    def __getitem__(self, idx):
        if idx is Ellipsis:
            return self.ref[self.row:self.row + 1, self.c0:self.c0 + self.width]
        cols = idx[1]
        return self.ref[self.row:self.row + 1, self.c0 + cols.start:self.c0 + cols.stop]


def _prompt_kernel(sinks_ref, x_ref, tab_all_ref, pm_ref, mtab_ref, vec_ref, win_ref, wdw_ref, wout_ref,
                   y_ref, nk_ref, nv_ref, nc_ref, kvar_ref, vt_ref, glu_ref, u_ref, z_ref):
    t = pl.program_id(1)
    nt = pl.num_programs(1)
    g_ref, gf_ref = _Row(vec_ref, 0), _Row(vec_ref, 1)
    bdw_ref, lng_ref, lnb_ref = _Row(vec_ref, 2, 0, D_CONV), _Row(vec_ref, 2, D_CONV, D_CONV), _Row(vec_ref, 3, 0, D_CONV)
    tab_ref = tab_all_ref.at[pl.ds(pl.multiple_of(t * TQ, TQ), TQ), :]

    @pl.when(t == 0)
    def _():
        pm = pm_ref[...]
        km = _rope(pm[:, K0:K0 + KV_WIDTH], *(mtab_ref[:, i * LANES:(i + 1) * LANES] for i in range(3)))
        vm = pm[:, V0:V0 + KV_WIDTH]
        invalid = jnp.zeros((BLOCK - N_META, LANES), f32)
        kvar_ref[:, 0:BLOCK - N_META, :] = jnp.zeros((4, BLOCK - N_META, LANES), bf16)
        for i, var in enumerate(_half_variants(km)):
            kvar_ref[i, BLOCK - N_META:BLOCK, :] = var
        for i, var in enumerate(_value_variants_t(jnp.concatenate([invalid, vm], axis=0))):
            vt_ref[i, :, 0:BLOCK] = var
        glu_ref[0:PAD - N_META, :] = jnp.zeros((PAD - N_META, D_CONV), f32)
        glu_ref[PAD - N_META:PAD, :] = pm[:, A0:A0 + D_CONV] * _sigmoid(pm[:, B0:B0 + D_CONV])

    u_ref[...] = _rms_norm(x_ref[0], g_ref[...]).astype(bf16)

    def proj(c0, width):
        return jnp.dot(u_ref[...], win_ref[:, c0:c0 + width], preferred_element_type=f32)

    cos, sp, sn, cos_q, sp_q, sn_q = (tab_ref[:, i * LANES:(i + 1) * LANES] for i in range(6))

    kv = proj(K0, 2 * KV_WIDTH)
    k = _rope(kv[:, 0:KV_WIDTH], cos, sp, sn)
    v = kv[:, KV_WIDTH:2 * KV_WIDTH]
    for i, var in enumerate(_half_variants(k)):
        kvar_ref[i, BLOCK:BLOCK + TQ, :] = var
    for j in range(TQ // BLOCK):
        for i, var in enumerate(_value_variants_t(v[j * BLOCK:(j + 1) * BLOCK])):
            vt_ref[i, :, (j + 1) * BLOCK:(j + 2) * BLOCK] = var

    def load_tile(s, i, lanes):
        return glu_ref[i * SUBLANES:(i + 1) * SUBLANES, lanes]

    def glu(c0, a, b):
        glu_ref[PAD:PAD + TQ, c0:c0 + PROJ_PIECE] = a * _sigmoid(b)

    conv_chain = [None, 0]

    def conv_piece(p):
        first = p * CONV_PIECE // LANES
        return jnp.concatenate([_conv_chunk(load_tile, 1, TQ // SUBLANES, wdw_ref, c, conv_chain)
                                for c in range(first, first + CONV_PIECE // LANES)], axis=1)

    def out_part(k0, width):
        lhs = z_ref[:, k0:k0 + width]
        return [jnp.dot(lhs, wout_ref[k0:k0 + width, c:c + OUT_PIECE], preferred_element_type=f32)
                for c in range(0, D_MODEL, OUT_PIECE)]

    def attention(kvh, q_raw, ga_raw):
        qh = _rope(q_raw, cos_q, sp_q, sn_q).astype(bf16)
        blocks = []
        for j in range(TQ // BLOCK):
            first_valid = jnp.where(t == 0, BLOCK - N_META, 0) if j == 0 else 0
            bias_t = _band_bias_t(first_valid)
            qb = qh[j * BLOCK:(j + 1) * BLOCK]
            acc = jnp.concatenate([_attn_piece(qb, kvar_ref, vt_ref, j * BLOCK, bias_t, sinks_ref, kvh, par)
                                   for par in range(2)], axis=0)
            blocks.append(jnp.concatenate([acc[:, i * BLOCK:(i + 1) * BLOCK].T for i in range(CHUNKS_PER_KV)], axis=1))
        z_attn = jnp.concatenate(blocks, axis=0) * _silu(ga_raw)
        z_ref[:, kvh * group:(kvh + 1) * group] = z_attn.astype(bf16)

    group = CHUNKS_PER_KV * LANES
    assert D_CONV == 2 * PROJ_PIECE and group == PROJ_PIECE and PROJ_PIECE == 2 * CONV_PIECE

    glu(0, proj(A0, PROJ_PIECE), proj(B0, PROJ_PIECE))
    a_hi = proj(A0 + PROJ_PIECE, PROJ_PIECE)
    conv = [conv_piece(0)]
    b_hi = proj(B0 + PROJ_PIECE, PROJ_PIECE)
    conv.append(conv_piece(1))
    glu(PROJ_PIECE, a_hi, b_hi)
    gc_lo = proj(GC0, PROJ_PIECE)
    conv.append(conv_piece(2))
    gc_hi = proj(GC0 + PROJ_PIECE, PROJ_PIECE)
    conv.append(conv_piece(3))
    q_lo = proj(Q0, group)
    z_conv = (_conv_ln(jnp.concatenate(conv, axis=1), bdw_ref, lng_ref, lnb_ref)
              * _silu(jnp.concatenate([gc_lo, gc_hi], axis=1)))
    z_ref[:, D_ATTN:D_ATTN + D_CONV] = z_conv.astype(bf16)

    q_hi = proj(Q0 + group, group)
    ga_lo = proj(GA0, group)
    attention(0, q_lo, ga_lo)
    ga_hi = proj(GA0 + group, group)
    o_conv = out_part(D_ATTN, D_CONV)
    attention(1, q_hi, ga_hi)
    o_lo = out_part(0, group)
    o_hi = out_part(group, group)

    hs, ssq = [], None
    for i, c in enumerate(range(0, D_MODEL, OUT_PIECE)):
        h = x_ref[0, :, c:c + OUT_PIECE] + (o_conv[i] + o_lo[i] + o_hi[i])
        part = jnp.sum(h * h, axis=-1, keepdims=True)
        ssq = part if ssq is None else ssq + part
        hs.append(h)
    scale = lax.rsqrt(ssq * (1.0 / D_MODEL) + EPS)
    for i, h in enumerate(hs):
        y_ref[0, :, i * OUT_PIECE:(i + 1) * OUT_PIECE] = h * scale * gf_ref[:, i * OUT_PIECE:(i + 1) * OUT_PIECE]

    @pl.when(t == nt - 1)
    def _():
        nk_ref[0] = k[TQ - WINDOW:TQ]
        nv_ref[0] = v[TQ - WINDOW:TQ]
        nc_ref[0] = glu_ref[PAD + TQ - PREFIX:PAD + TQ, :]

    kvar_ref[:, 0:BLOCK, :] = kvar_ref[:, TQ:TQ + BLOCK, :]
    vt_ref[:, :, 0:BLOCK] = vt_ref[:, :, TQ:TQ + BLOCK]
    glu_ref[0:PAD, :] = glu_ref[TQ:TQ + PAD, :]


def _prompt_call(sinks, x, tabs, p_meta, mtabs, g, w_in, w_dw, b_dw, ln_g, ln_b, w_out, gf):
    nb, seq, _ = x.shape
    nt = seq // TQ
    once = pl.Buffered(1)

    def const(shape):
        return pl.BlockSpec(shape, lambda b, t: (0,) * len(shape), pipeline_mode=once)

    q_scale = HEAD_DIM ** -0.5
    tab = jnp.concatenate(list(tabs) + [tb * q_scale for tb in tabs], axis=1)
    mtab = jnp.concatenate(mtabs, axis=1)
    pad = jnp.zeros((1, D_MODEL - D_CONV), f32)
    vecs = jnp.concatenate([g, gf, jnp.concatenate([b_dw, ln_g], axis=1), jnp.concatenate([ln_b, pad], axis=1),
                            jnp.zeros((SUBLANES - 4, D_MODEL), f32)], axis=0)
    return pl.pallas_call(
        _prompt_kernel,
        grid=(nb, nt),
        in_specs=[
            pl.BlockSpec(memory_space=pltpu.SMEM),
            pl.BlockSpec((1, TQ, D_MODEL), lambda b, t: (b, t, 0)),
            const((seq, 6 * LANES)),
            const((N_META, D_PROJ)), const((N_META, 3 * LANES)), const((SUBLANES, D_MODEL)),
            const((D_MODEL, D_PROJ)), const((CONV_WIDTH, D_CONV)), const((D_ATTN + D_CONV, D_MODEL)),
        ],
        out_specs=[
            pl.BlockSpec((1, TQ, D_MODEL), lambda b, t: (b, t, 0)),
            pl.BlockSpec((1, WINDOW, KV_WIDTH), lambda b, t: (b, 0, 0)),
            pl.BlockSpec((1, WINDOW, KV_WIDTH), lambda b, t: (b, 0, 0)),
            pl.BlockSpec((1, PREFIX, D_CONV), lambda b, t: (b, 0, 0)),
        ],
        out_shape=[
            jax.ShapeDtypeStruct((nb, seq, D_MODEL), f32),
            jax.ShapeDtypeStruct((nb, WINDOW, KV_WIDTH), f32),
            jax.ShapeDtypeStruct((nb, WINDOW, KV_WIDTH), f32),
            jax.ShapeDtypeStruct((nb, PREFIX, D_CONV), f32),
        ],
        scratch_shapes=[
            pltpu.VMEM((4, BLOCK + TQ, LANES), bf16),
            pltpu.VMEM((4, LANES, BLOCK + TQ), bf16),
            pltpu.VMEM((PAD + TQ, D_CONV), f32),
            pltpu.VMEM((TQ, D_MODEL), bf16),
            pltpu.VMEM((TQ, D_ATTN + D_CONV), bf16),
        ],
        compiler_params=pltpu.CompilerParams(
            dimension_semantics=("arbitrary", "arbitrary"), vmem_limit_bytes=VMEM_LIMIT),
    )(sinks, x, tab, p_meta, mtab, vecs, w_in, w_dw, w_out)


SEQ_BLOCK = 16
SEQ_UNROLL = 4
DEC = 8
KEYS = 2 * WINDOW


def _sample_mixer_kernel(p_ref, x_ref, ck_ref, cv_ref, st_ref, cos_ref, sp_ref, sn_ref, sink_ref,
                         wdw_ref, bdw_ref, lng_ref, lnb_ref, wout_ref, gf_ref,
                         y_ref, nk_ref, nv_ref, nc_ref, q_scr, k_scr, v_scr, attn_scr, cbuf):
    rows = SEQ_BLOCK * DEC
    cos, sp, sn = cos_ref[...], sp_ref[...], sn_ref[...]

    k = _rope(p_ref[:, K0:K0 + KV_WIDTH], cos, sp, sn)
    v = p_ref[:, V0:V0 + KV_WIDTH]
    q_scr[...] = _rope(p_ref[:, Q0:Q0 + D_ATTN], cos, sp, sn) * (HEAD_DIM ** -0.5)
    k_scr[...] = k
    v_scr[...] = v

    nk_ref[:, 0:WINDOW - DEC, :] = ck_ref[:, DEC:WINDOW, :]
    nk_ref[:, WINDOW - DEC:WINDOW, :] = k.reshape(SEQ_BLOCK, DEC, KV_WIDTH)
    nv_ref[:, 0:WINDOW - DEC, :] = cv_ref[:, DEC:WINDOW, :]
    nv_ref[:, WINDOW - DEC:WINDOW, :] = v.reshape(SEQ_BLOCK, DEC, KV_WIDTH)

    glu = p_ref[:, A0:A0 + D_CONV] * _sigmoid(p_ref[:, B0:B0 + D_CONV])
    cbuf[:, 0:SUBLANES, :] = jnp.zeros((SEQ_BLOCK, SUBLANES, D_CONV), f32)
    cbuf[:, CONV_OFF:PAD, :] = st_ref[...]
    cbuf[:, PAD:PAD + DEC, :] = glu.reshape(SEQ_BLOCK, DEC, D_CONV)
    nc_ref[...] = cbuf[:, PAD + DEC - PREFIX:PAD + DEC, :]
    conv = _causal_conv(lambda s, j, lanes: cbuf[s, j * SUBLANES:(j + 1) * SUBLANES, lanes],
                        SEQ_BLOCK, DEC // SUBLANES, wdw_ref)
    z_conv = _conv_ln(conv, bdw_ref, lng_ref, lnb_ref) * _silu(p_ref[:, GC0:GC0 + D_CONV])
    o_conv = jnp.dot(z_conv.astype(bf16), wout_ref[D_ATTN:D_ATTN + D_CONV, :], preferred_element_type=f32)

    row = lax.broadcasted_iota(jnp.int32, (N_HEADS * DEC, KEYS), 0) % DEC
    col = lax.broadcasted_iota(jnp.int32, (N_HEADS * DEC, KEYS), 1)
    valid = ((col < WINDOW) & (col >= row)) | ((col >= WINDOW) & (col - WINDOW <= row))
    bias = jnp.where(valid, 0.0, NEG).astype(f32)
    lo = lax.broadcasted_iota(jnp.int32, (DEC, LANES), 1) < HEAD_DIM
    zero = jnp.zeros((DEC, LANES), f32)
    pad_rows = jnp.zeros((KEYS - WINDOW - DEC, LANES), f32)
    sink = sink_ref[...]

    def body(s, carry):
        r0 = pl.multiple_of(s * DEC, DEC)
        qs = q_scr[pl.ds(r0, DEC), :]
        pieces = [None] * N_HEADS
        for c in range(CHUNKS):
            qc = qs[:, c * LANES:(c + 1) * LANES]
            qr = pltpu.roll(qc, HEAD_DIM, 1)
            if c < CHUNKS_PER_KV:
                pieces[2 * c], pieces[2 * c + 1] = jnp.where(lo, qc, zero), jnp.where(lo, qr, zero)
            else:
                pieces[2 * c], pieces[2 * c + 1] = jnp.where(lo, zero, qr), jnp.where(lo, zero, qc)
        qrows = jnp.concatenate(pieces, axis=0).astype(bf16)
        kf = jnp.concatenate([ck_ref[s], k_scr[pl.ds(r0, DEC), :], pad_rows], axis=0).astype(bf16)
        vf = jnp.concatenate([cv_ref[s], v_scr[pl.ds(r0, DEC), :], pad_rows], axis=0).astype(bf16)
        sc = lax.dot_general(qrows, kf, (((1,), (1,)), ((), ())), preferred_element_type=f32) + bias
        m = jnp.maximum(jnp.broadcast_to(jnp.max(sc, axis=1, keepdims=True), sink.shape), sink)
        e = jnp.exp(sc - jnp.concatenate([m] * (KEYS // LANES), axis=1))
        den = jnp.broadcast_to(jnp.sum(e, axis=1, keepdims=True), sink.shape) + jnp.exp(sink - m)
        o = jnp.dot(e.astype(bf16), vf, preferred_element_type=f32) * (1.0 / den)
        outs = []
        for c in range(CHUNKS):
            ev, od = o[2 * c * DEC:(2 * c + 1) * DEC], o[(2 * c + 1) * DEC:(2 * c + 2) * DEC]
            if c < CHUNKS_PER_KV:
                outs.append(jnp.where(lo, ev, pltpu.roll(od, HEAD_DIM, 1)))
            else:
                outs.append(jnp.where(lo, pltpu.roll(ev, HEAD_DIM, 1), od))
        attn_scr[pl.ds(r0, DEC), :] = jnp.concatenate(outs, axis=1)
        return carry

    lax.fori_loop(0, SEQ_BLOCK, body, 0, unroll=SEQ_UNROLL)

    z_attn = attn_scr[...] * _silu(p_ref[:, GA0:GA0 + D_ATTN])
    o_attn = jnp.dot(z_attn.astype(bf16), wout_ref[0:D_ATTN, :], preferred_element_type=f32)
    y_ref[...] = _rms_norm(x_ref[...] + (o_conv + o_attn), gf_ref[...])


def _sample_mixer(p, x, ck, cv, st, tabs, sink_col, w_dw, b_dw, ln_g, ln_b, w_out, gf):
    nseq = ck.shape[0]
    rows = SEQ_BLOCK * DEC

    def const(shape):
        return pl.BlockSpec(shape, lambda i: (0,) * len(shape))

    return pl.pallas_call(
        _sample_mixer_kernel,
        grid=(nseq // SEQ_BLOCK,),
        in_specs=[
            pl.BlockSpec((rows, D_PROJ), lambda i: (i, 0)),
            pl.BlockSpec((rows, D_MODEL), lambda i: (i, 0)),
            pl.BlockSpec((SEQ_BLOCK, WINDOW, KV_WIDTH), lambda i: (i, 0, 0)),
            pl.BlockSpec((SEQ_BLOCK, WINDOW, KV_WIDTH), lambda i: (i, 0, 0)),
            pl.BlockSpec((SEQ_BLOCK, PREFIX, D_CONV), lambda i: (i, 0, 0)),
            const((rows, LANES)), const((rows, LANES)), const((rows, LANES)),
            const((N_HEADS * DEC, LANES)),
            const((CONV_WIDTH, D_CONV)), const((1, D_CONV)), const((1, D_CONV)), const((1, D_CONV)),
            pl.BlockSpec((D_ATTN + D_CONV, D_MODEL), lambda i: (0, 0), pipeline_mode=pl.Buffered(1)),
            const((1, D_MODEL)),
        ],
        out_specs=[
            pl.BlockSpec((rows, D_MODEL), lambda i: (i, 0)),
            pl.BlockSpec((SEQ_BLOCK, WINDOW, KV_WIDTH), lambda i: (i, 0, 0)),
            pl.BlockSpec((SEQ_BLOCK, WINDOW, KV_WIDTH), lambda i: (i, 0, 0)),
            pl.BlockSpec((SEQ_BLOCK, PREFIX, D_CONV), lambda i: (i, 0, 0)),
        ],
        out_shape=[
            jax.ShapeDtypeStruct((nseq * DEC, D_MODEL), f32),
            jax.ShapeDtypeStruct((nseq, WINDOW, KV_WIDTH), f32),
            jax.ShapeDtypeStruct((nseq, WINDOW, KV_WIDTH), f32),
            jax.ShapeDtypeStruct((nseq, PREFIX, D_CONV), f32),
        ],
        scratch_shapes=[
            pltpu.VMEM((rows, D_ATTN), f32),
            pltpu.VMEM((rows, KV_WIDTH), f32),
            pltpu.VMEM((rows, KV_WIDTH), f32),
            pltpu.VMEM((rows, D_ATTN), f32),
            pltpu.VMEM((SEQ_BLOCK, PAD + DEC, D_CONV), f32),
        ],
        compiler_params=pltpu.CompilerParams(dimension_semantics=("arbitrary",), vmem_limit_bytes=VMEM_LIMIT),
    )(p, x, ck, cv, st, *tabs, sink_col, w_dw, b_dw, ln_g, ln_b, w_out, gf)


def kernel(x_prompt, x_sample, cache_k_win, cache_v_win, state_conv, meta_tokens, norm_gain, w_in,
           attn_sinks, w_dw, b_dw, conv_norm_gain, conv_norm_bias, w_out, final_norm_gain):
    nb, seq, _ = x_prompt.shape
    nseq, dec, _ = x_sample.shape
    assert norm_gain.shape[0] == 1 and dec == DEC and seq % TQ == 0 and nseq % SEQ_BLOCK == 0

    g = norm_gain[0][None]
    w_out_b = w_out[0].astype(bf16)
    sinks = attn_sinks[0]
    wdw, bdw = w_dw[0], b_dw[0][None]
    lng, lnb = conv_norm_gain[0][None], conv_norm_bias[0][None]
    gf = final_norm_gain[None]

    xs = x_sample.reshape(nseq * dec, D_MODEL)
    p_s, p_meta, w_in_b = _project_sample(xs, meta_tokens, g, w_in[0])

    tabs = _rope_tables(N_META + jnp.arange(seq))
    mtabs = _rope_tables(jnp.arange(N_META))
    y_prompt, nk_p, nv_p, nc_p = _prompt_call(sinks, x_prompt, tabs, p_meta, mtabs, g, w_in_b,
                                               wdw, bdw, lng, lnb, w_out_b, gf)

    stabs = _rope_tables(PAST_LEN + jnp.arange(SEQ_BLOCK * dec) % dec)
    sink_col = jnp.broadcast_to(jnp.repeat(sinks, dec)[:, None], (N_HEADS * dec, LANES))
    ck = cache_k_win[0].reshape(nseq, WINDOW, KV_WIDTH)
    cv = cache_v_win[0].reshape(nseq, WINDOW, KV_WIDTH)
    y_s, nk_s, nv_s, nc_s = _sample_mixer(p_s, xs, ck, cv, state_conv[0], stabs, sink_col, wdw, bdw, lng, lnb,
                                          w_out_b, gf)
    y_sample = y_s.reshape(nseq, dec, D_MODEL)

    kv_shape = (1, -1, WINDOW, N_KV_HEADS, HEAD_DIM)
    return (y_prompt, y_sample, nk_p.reshape(kv_shape), nv_p.reshape(kv_shape), nc_p[None],
            nk_s.reshape(kv_shape), nv_s.reshape(kv_shape), nc_s[None])
```

```python
import jax
import jax.numpy as jnp
from jax import lax
from jax.experimental import pallas as pl
from jax.experimental.pallas import tpu as pltpu

D_MODEL = 2048
N_META = 16
D_ATTN = 1024
D_CONV = 1024
HEAD_DIM = 64
N_HEADS = 16
N_KV_HEADS = 2
KV_WIDTH = N_KV_HEADS * HEAD_DIM
WINDOW = 128
BLOCK = 128
ROPE_DIM = 16
ROPE_THETA = 500000.0
CONV_WIDTH = 31
PREFIX = CONV_WIDTH - 1
D_PROJ = 2 * D_ATTN + 2 * KV_WIDTH + 3 * D_CONV
EPS = 1e-5
PAST_LEN = 8192

Q0 = 0
K0 = D_ATTN
V0 = K0 + KV_WIDTH
GA0 = V0 + KV_WIDTH
A0 = GA0 + D_ATTN
B0 = A0 + D_CONV
GC0 = B0 + D_CONV

LANES = 128
SUBLANES = 8
CHUNKS = D_ATTN // LANES
CHUNKS_PER_KV = CHUNKS // N_KV_HEADS
PAD = 32
NEG = -1e30

TQ = 256
VMEM_LIMIT = 56 * 1024 * 1024

f32 = jnp.float32
bf16 = jnp.bfloat16


def _rms_norm(x, g):
    return x * lax.rsqrt(jnp.mean(x * x, axis=-1, keepdims=True) + EPS) * g


NEG_LOG2_E = -1.4426950408889634


def _sigmoid(x):
    return 1.0 / (1.0 + jnp.exp2(x * NEG_LOG2_E))


def _silu(x):
    return x * _sigmoid(x)


def _rope(x, cos, sin_prev, sin_next):
    outs = []
    for c in range(x.shape[1] // LANES):
        xc = x[:, c * LANES:(c + 1) * LANES]
        outs.append(xc * cos + pltpu.roll(xc, 8, 1) * sin_prev + pltpu.roll(xc, LANES - 8, 1) * sin_next)
    return outs[0] if len(outs) == 1 else jnp.concatenate(outs, axis=1)


def _half_variants(x):
    lo = lax.broadcasted_iota(jnp.int32, x.shape, 1) < HEAD_DIM
    xr = pltpu.roll(x, HEAD_DIM, 1)
    zero = jnp.zeros_like(x)
    return (jnp.where(lo, x, zero).astype(bf16), jnp.where(lo, zero, xr).astype(bf16),
            jnp.where(lo, xr, zero).astype(bf16), jnp.where(lo, zero, x).astype(bf16))


def _value_variants_t(v):
    vt = v.T
    top, bot = vt[0:HEAD_DIM], vt[HEAD_DIM:2 * HEAD_DIM]
    ones = jnp.ones_like(top)
    return tuple(jnp.concatenate(pair, axis=0).astype(bf16)
                 for pair in ((top, ones), (ones, top), (bot, ones), (ones, bot)))


def _layer_norm(x, g, b):
    mu = jnp.mean(x, axis=-1, keepdims=True)
    xc = x - mu
    return xc * lax.rsqrt(jnp.mean(xc * xc, axis=-1, keepdims=True) + EPS) * g + b


def _rope_tables(pos):
    half = ROPE_DIM // 2
    inv = ROPE_THETA ** (-jnp.arange(0, ROPE_DIM, 2, dtype=f32) / ROPE_DIM)
    ang = pos.astype(f32)[:, None] * inv[None, :]
    cos, sin = jnp.cos(ang), jnp.sin(ang)
    t = pos.shape[0]
    ones = jnp.ones((t, HEAD_DIM - ROPE_DIM), f32)
    zeros = jnp.zeros((t, HEAD_DIM - ROPE_DIM), f32)
    zh = jnp.zeros((t, half), f32)
    cos64 = jnp.concatenate([cos, cos, ones], axis=1)
    prev64 = jnp.concatenate([zh, sin, zeros], axis=1)
    next64 = jnp.concatenate([-sin, zh, zeros], axis=1)
    rep = LANES // HEAD_DIM
    return jnp.tile(cos64, (1, rep)), jnp.tile(prev64, (1, rep)), jnp.tile(next64, (1, rep))


PROJ_BN = 768


def _proj_kernel(x_ref, meta_ref, g_ref, w_ref, p_ref, pm_ref, wb_ref, u_ref):
    n = x_ref.shape[0]

    @pl.when(pl.program_id(0) == 0)
    def _():
        u_ref[0:n, :] = _rms_norm(x_ref[...], g_ref[...]).astype(bf16)
        u_ref[n:n + N_META, :] = _rms_norm(meta_ref[...], g_ref[...]).astype(bf16)

    w = w_ref[...].astype(bf16)
    wb_ref[...] = w
    p = jnp.dot(u_ref[...], w, preferred_element_type=f32)
    p_ref[...] = p[0:n]
    pm_ref[...] = p[n:n + N_META]


def _project_sample(x, meta, g, w_in):
    n = x.shape[0]
    once = pl.Buffered(1)
    return pl.pallas_call(
        _proj_kernel,
        grid=(D_PROJ // PROJ_BN,),
        in_specs=[
            pl.BlockSpec((n, D_MODEL), lambda j: (0, 0), pipeline_mode=once),
            pl.BlockSpec((N_META, D_MODEL), lambda j: (0, 0), pipeline_mode=once),
            pl.BlockSpec((1, D_MODEL), lambda j: (0, 0), pipeline_mode=once),
            pl.BlockSpec((D_MODEL, PROJ_BN), lambda j: (0, j)),
        ],
        out_specs=[
            pl.BlockSpec((n, PROJ_BN), lambda j: (0, j)),
            pl.BlockSpec((N_META, PROJ_BN), lambda j: (0, j)),
            pl.BlockSpec((D_MODEL, PROJ_BN), lambda j: (0, j)),
        ],
        out_shape=[jax.ShapeDtypeStruct((n, D_PROJ), f32), jax.ShapeDtypeStruct((N_META, D_PROJ), f32),
                   jax.ShapeDtypeStruct((D_MODEL, D_PROJ), bf16)],
        scratch_shapes=[pltpu.VMEM((n + N_META, D_MODEL), bf16)],
        compiler_params=pltpu.CompilerParams(dimension_semantics=("arbitrary",), vmem_limit_bytes=VMEM_LIMIT),
    )(x, meta, g, w_in)


CONV_OFF = PAD - PREFIX
CONV_TAP_TILES = (CONV_OFF + CONV_WIDTH - 1) // SUBLANES + 1


CHAIN_ROLL_EVERY = 8


def _chain_link(chain, y):
    if chain[0] is not None:
        y = y + chain[0]
    token = jnp.minimum(jnp.abs(y), 0.0)
    chain[1] += 1
    if chain[1] % CHAIN_ROLL_EVERY == 0:
        token = pltpu.roll(token, 1, 1)
    chain[0] = token
    return y


def _conv_chunk(load_tile, n_seq, n_tiles, wdw_ref, c, chain=None):
    row = lax.broadcasted_iota(jnp.int32, (SUBLANES, LANES), 0)
    lanes = slice(c * LANES, (c + 1) * LANES)
    w = [jnp.broadcast_to(wdw_ref[k:k + 1, lanes], (SUBLANES, LANES)) for k in range(CONV_WIDTH)]
    outs = []
    for s in range(n_seq):
        g = [load_tile(s, j, lanes) for j in range(n_tiles + CONV_TAP_TILES - 1)]

        def z(r, i):
            acc = None
            for a in range(CONV_TAP_TILES):
                k = SUBLANES * a + r - CONV_OFF
                if 0 <= k < CONV_WIDTH:
                    term = w[k] * g[i + a]
                    acc = term if acc is None else acc + term
            return acc

        prev = [z(r, 0) for r in range(1, SUBLANES)]
        for i in range(n_tiles):
            nxt = [z(r, i + 1) for r in range(1, SUBLANES)]
            y = z(0, i)
            for r in range(1, SUBLANES):
                y = y + pltpu.roll(jnp.where(row >= r, prev[r - 1], nxt[r - 1]), SUBLANES - r, 0)
            outs.append(y if chain is None else _chain_link(chain, y))
            prev = nxt
    return jnp.concatenate(outs, axis=0)


def _causal_conv(load_tile, n_seq, n_tiles, wdw_ref):
    return jnp.concatenate([_conv_chunk(load_tile, n_seq, n_tiles, wdw_ref, c)
                            for c in range(D_CONV // LANES)], axis=1)


def _conv_ln(y, bdw_ref, lng_ref, lnb_ref):
    return _silu(_layer_norm(y + bdw_ref[...], lng_ref[...], lnb_ref[...]))


def _band_bias_t(first_valid_key):
    key = lax.broadcasted_iota(jnp.int32, (2 * BLOCK, BLOCK), 0)
    qry = lax.broadcasted_iota(jnp.int32, (2 * BLOCK, BLOCK), 1)
    valid = (key >= qry) & (key <= qry + WINDOW) & (key >= first_valid_key)
    return jnp.where(valid, 0.0, NEG).astype(f32)


def _attn_piece(q_blk, kvar_ref, vt_ref, row0, bias_t, sinks_ref, kv, par):
    qs = jnp.concatenate([q_blk[:, i * LANES:(i + 1) * LANES] for i in range(CHUNKS_PER_KV)], axis=0)
    kk = kvar_ref[kv * 2 + par, row0:row0 + 2 * BLOCK, :]
    vt = vt_ref[kv * 2 + par, :, row0:row0 + 2 * BLOCK]
    st = lax.dot_general(kk, qs, (((1,), (1,)), ((), ())), preferred_element_type=f32)
    es, sink_terms = [], []
    for i in range(CHUNKS_PER_KV):
        blk = st[:, i * BLOCK:(i + 1) * BLOCK] + bias_t
        sink = sinks_ref[kv * 8 + 2 * i + par]
        m = jnp.maximum(jnp.max(blk, axis=0, keepdims=True), sink)
        es.append(jnp.exp(blk - m).astype(bf16))
        sink_terms.append(jnp.exp(sink - m))
    out_t = jnp.dot(vt, jnp.concatenate(es, axis=1), preferred_element_type=f32)
    vals, sums = (0, HEAD_DIM) if par == 0 else (HEAD_DIM, 0)
    den = out_t[sums:sums + 1, :] + jnp.concatenate(sink_terms, axis=1)
    return out_t[vals:vals + HEAD_DIM, :] * (1.0 / den)


PROJ_PIECE = 512
OUT_PIECE = 512
CONV_PIECE = 256


def _prompt_kernel(sinks_ref, x_ref, tab_ref, pm_ref, mcos_ref, msp_ref, msn_ref,
                   g_ref, win_ref, wdw_ref, bdw_ref, lng_ref, lnb_ref, wout_ref, gf_ref,
                   y_ref, nk_ref, nv_ref, nc_ref, kvar_ref, vt_ref, glu_ref, u_ref, z_ref):
    t = pl.program_id(1)
    nt = pl.num_programs(1)

    @pl.when(t == 0)
    def _():
        pm = pm_ref[...]
        km = _rope(pm[:, K0:K0 + KV_WIDTH], mcos_ref[...], msp_ref[...], msn_ref[...])
        vm = pm[:, V0:V0 + KV_WIDTH]
        invalid = jnp.zeros((BLOCK - N_META, LANES), f32)
        kvar_ref[:, 0:BLOCK - N_META, :] = jnp.zeros((4, BLOCK - N_META, LANES), bf16)
        for i, var in enumerate(_half_variants(km)):
            kvar_ref[i, BLOCK - N_META:BLOCK, :] = var
        for i, var in enumerate(_value_variants_t(jnp.concatenate([invalid, vm], axis=0))):
            vt_ref[i, :, 0:BLOCK] = var
        glu_ref[0:PAD - N_META, :] = jnp.zeros((PAD - N_META, D_CONV), f32)
        glu_ref[PAD - N_META:PAD, :] = pm[:, A0:A0 + D_CONV] * _sigmoid(pm[:, B0:B0 + D_CONV])

    u_ref[...] = _rms_norm(x_ref[0], g_ref[...]).astype(bf16)

    def proj(c0, width):
        return jnp.dot(u_ref[...], win_ref[:, c0:c0 + width], preferred_element_type=f32)

    cos, sp, sn, cos_q, sp_q, sn_q = (tab_ref[:, i * LANES:(i + 1) * LANES] for i in range(6))

    kv = proj(K0, 2 * KV_WIDTH)
    k = _rope(kv[:, 0:KV_WIDTH], cos, sp, sn)
    v = kv[:, KV_WIDTH:2 * KV_WIDTH]
    for i, var in enumerate(_half_variants(k)):
        kvar_ref[i, BLOCK:BLOCK + TQ, :] = var
    for j in range(TQ // BLOCK):
        for i, var in enumerate(_value_variants_t(v[j * BLOCK:(j + 1) * BLOCK])):
            vt_ref[i, :, (j + 1) * BLOCK:(j + 2) * BLOCK] = var

    def load_tile(s, i, lanes):
        return glu_ref[i * SUBLANES:(i + 1) * SUBLANES, lanes]

    def glu(c0, a, b):
        glu_ref[PAD:PAD + TQ, c0:c0 + PROJ_PIECE] = a * _sigmoid(b)

    conv_chain = [None, 0]

    def conv_piece(p):
        first = p * CONV_PIECE // LANES
        return jnp.concatenate([_conv_chunk(load_tile, 1, TQ // SUBLANES, wdw_ref, c, conv_chain)
                                for c in range(first, first + CONV_PIECE // LANES)], axis=1)

    def out_part(k0, width):
        lhs = z_ref[:, k0:k0 + width]
        return [jnp.dot(lhs, wout_ref[k0:k0 + width, c:c + OUT_PIECE], preferred_element_type=f32)
                for c in range(0, D_MODEL, OUT_PIECE)]

    def attention(kvh, q_raw, ga_raw):
        qh = _rope(q_raw, cos_q, sp_q, sn_q).astype(bf16)
        blocks = []
        for j in range(TQ // BLOCK):
            first_valid = jnp.where(t == 0, BLOCK - N_META, 0) if j == 0 else 0
            bias_t = _band_bias_t(first_valid)
            qb = qh[j * BLOCK:(j + 1) * BLOCK]
            acc = jnp.concatenate([_attn_piece(qb, kvar_ref, vt_ref, j * BLOCK, bias_t, sinks_ref, kvh, par)
                                   for par in range(2)], axis=0)
            blocks.append(jnp.concatenate([acc[:, i * BLOCK:(i + 1) * BLOCK].T for i in range(CHUNKS_PER_KV)], axis=1))
        z_attn = jnp.concatenate(blocks, axis=0) * _silu(ga_raw)
        z_ref[:, kvh * group:(kvh + 1) * group] = z_attn.astype(bf16)

    group = CHUNKS_PER_KV * LANES
    assert D_CONV == 2 * PROJ_PIECE and group == PROJ_PIECE and PROJ_PIECE == 2 * CONV_PIECE

    glu(0, proj(A0, PROJ_PIECE), proj(B0, PROJ_PIECE))
    a_hi = proj(A0 + PROJ_PIECE, PROJ_PIECE)
    conv = [conv_piece(0)]
    b_hi = proj(B0 + PROJ_PIECE, PROJ_PIECE)
    conv.append(conv_piece(1))
    glu(PROJ_PIECE, a_hi, b_hi)
    gc_lo = proj(GC0, PROJ_PIECE)
    conv.append(conv_piece(2))
    gc_hi = proj(GC0 + PROJ_PIECE, PROJ_PIECE)
    conv.append(conv_piece(3))
    q_lo = proj(Q0, group)
    z_conv = (_conv_ln(jnp.concatenate(conv, axis=1), bdw_ref, lng_ref, lnb_ref)
              * _silu(jnp.concatenate([gc_lo, gc_hi], axis=1)))
    z_ref[:, D_ATTN:D_ATTN + D_CONV] = z_conv.astype(bf16)

    q_hi = proj(Q0 + group, group)
    ga_lo = proj(GA0, group)
    attention(0, q_lo, ga_lo)
    ga_hi = proj(GA0 + group, group)
    o_conv = out_part(D_ATTN, D_CONV)
    attention(1, q_hi, ga_hi)
    o_lo = out_part(0, group)
    o_hi = out_part(group, group)

    hs, ssq = [], None
    for i, c in enumerate(range(0, D_MODEL, OUT_PIECE)):
        h = x_ref[0, :, c:c + OUT_PIECE] + (o_conv[i] + o_lo[i] + o_hi[i])
        part = jnp.sum(h * h, axis=-1, keepdims=True)
        ssq = part if ssq is None else ssq + part
        hs.append(h)
    scale = lax.rsqrt(ssq * (1.0 / D_MODEL) + EPS)
    for i, h in enumerate(hs):
        y_ref[0, :, i * OUT_PIECE:(i + 1) * OUT_PIECE] = h * scale * gf_ref[:, i * OUT_PIECE:(i + 1) * OUT_PIECE]

    @pl.when(t == nt - 1)
    def _():
        nk_ref[0] = k[TQ - WINDOW:TQ]
        nv_ref[0] = v[TQ - WINDOW:TQ]
        nc_ref[0] = glu_ref[PAD + TQ - PREFIX:PAD + TQ, :]

    kvar_ref[:, 0:BLOCK, :] = kvar_ref[:, TQ:TQ + BLOCK, :]
    vt_ref[:, :, 0:BLOCK] = vt_ref[:, :, TQ:TQ + BLOCK]
    glu_ref[0:PAD, :] = glu_ref[TQ:TQ + PAD, :]


def _prompt_call(sinks, x, tabs, p_meta, mtabs, g, w_in, w_dw, b_dw, ln_g, ln_b, w_out, gf):
    nb, seq, _ = x.shape
    nt = seq // TQ
    once = pl.Buffered(1)

    def const(shape):
        return pl.BlockSpec(shape, lambda b, t: (0,) * len(shape), pipeline_mode=once)

    q_scale = HEAD_DIM ** -0.5
    tab = jnp.concatenate(list(tabs) + [tb * q_scale for tb in tabs], axis=1)
    tab_spec = pl.BlockSpec((TQ, 6 * LANES), lambda b, t: (t, 0))
    return pl.pallas_call(
        _prompt_kernel,
        grid=(nb, nt),
        in_specs=[
            pl.BlockSpec(memory_space=pltpu.SMEM),
            pl.BlockSpec((1, TQ, D_MODEL), lambda b, t: (b, t, 0)),
            tab_spec,
            const((N_META, D_PROJ)), const((N_META, LANES)), const((N_META, LANES)), const((N_META, LANES)),
            const((1, D_MODEL)), const((D_MODEL, D_PROJ)),
            const((CONV_WIDTH, D_CONV)), const((1, D_CONV)), const((1, D_CONV)), const((1, D_CONV)),
            const((D_ATTN + D_CONV, D_MODEL)), const((1, D_MODEL)),
        ],
        out_specs=[
            pl.BlockSpec((1, TQ, D_MODEL), lambda b, t: (b, t, 0)),
            pl.BlockSpec((1, WINDOW, KV_WIDTH), lambda b, t: (b, 0, 0)),
            pl.BlockSpec((1, WINDOW, KV_WIDTH), lambda b, t: (b, 0, 0)),
            pl.BlockSpec((1, PREFIX, D_CONV), lambda b, t: (b, 0, 0)),
        ],
        out_shape=[
            jax.ShapeDtypeStruct((nb, seq, D_MODEL), f32),
            jax.ShapeDtypeStruct((nb, WINDOW, KV_WIDTH), f32),
            jax.ShapeDtypeStruct((nb, WINDOW, KV_WIDTH), f32),
            jax.ShapeDtypeStruct((nb, PREFIX, D_CONV), f32),
        ],
        scratch_shapes=[
            pltpu.VMEM((4, BLOCK + TQ, LANES), bf16),
            pltpu.VMEM((4, LANES, BLOCK + TQ), bf16),
            pltpu.VMEM((PAD + TQ, D_CONV), f32),
            pltpu.VMEM((TQ, D_MODEL), bf16),
            pltpu.VMEM((TQ, D_ATTN + D_CONV), bf16),
        ],
        compiler_params=pltpu.CompilerParams(
            dimension_semantics=("parallel", "arbitrary"), vmem_limit_bytes=VMEM_LIMIT),
    )(sinks, x, tab, p_meta, *mtabs, g, w_in, w_dw, b_dw, ln_g, ln_b, w_out, gf)


SEQ_BLOCK = 16
SEQ_UNROLL = 4
DEC = 8
KEYS = 2 * WINDOW


def _sample_mixer_kernel(p_ref, x_ref, ck_ref, cv_ref, st_ref, cos_ref, sp_ref, sn_ref, sink_ref,
                         wdw_ref, bdw_ref, lng_ref, lnb_ref, wout_ref, gf_ref,
                         y_ref, nk_ref, nv_ref, nc_ref, q_scr, k_scr, v_scr, attn_scr, cbuf):
    rows = SEQ_BLOCK * DEC
    cos, sp, sn = cos_ref[...], sp_ref[...], sn_ref[...]

    k = _rope(p_ref[:, K0:K0 + KV_WIDTH], cos, sp, sn)
    v = p_ref[:, V0:V0 + KV_WIDTH]
    q_scr[...] = _rope(p_ref[:, Q0:Q0 + D_ATTN], cos, sp, sn) * (HEAD_DIM ** -0.5)
    k_scr[...] = k
    v_scr[...] = v

    nk_ref[:, 0:WINDOW - DEC, :] = ck_ref[:, DEC:WINDOW, :]
    nk_ref[:, WINDOW - DEC:WINDOW, :] = k.reshape(SEQ_BLOCK, DEC, KV_WIDTH)
    nv_ref[:, 0:WINDOW - DEC, :] = cv_ref[:, DEC:WINDOW, :]
    nv_ref[:, WINDOW - DEC:WINDOW, :] = v.reshape(SEQ_BLOCK, DEC, KV_WIDTH)

    glu = p_ref[:, A0:A0 + D_CONV] * _sigmoid(p_ref[:, B0:B0 + D_CONV])
    cbuf[:, 0:SUBLANES, :] = jnp.zeros((SEQ_BLOCK, SUBLANES, D_CONV), f32)
    cbuf[:, CONV_OFF:PAD, :] = st_ref[...]
    cbuf[:, PAD:PAD + DEC, :] = glu.reshape(SEQ_BLOCK, DEC, D_CONV)
    nc_ref[...] = cbuf[:, PAD + DEC - PREFIX:PAD + DEC, :]
    conv = _causal_conv(lambda s, j, lanes: cbuf[s, j * SUBLANES:(j + 1) * SUBLANES, lanes],
                        SEQ_BLOCK, DEC // SUBLANES, wdw_ref)
    z_conv = _conv_ln(conv, bdw_ref, lng_ref, lnb_ref) * _silu(p_ref[:, GC0:GC0 + D_CONV])
    o_conv = jnp.dot(z_conv.astype(bf16), wout_ref[D_ATTN:D_ATTN + D_CONV, :], preferred_element_type=f32)

    row = lax.broadcasted_iota(jnp.int32, (N_HEADS * DEC, KEYS), 0) % DEC
    col = lax.broadcasted_iota(jnp.int32, (N_HEADS * DEC, KEYS), 1)
    valid = ((col < WINDOW) & (col >= row)) | ((col >= WINDOW) & (col - WINDOW <= row))
    bias = jnp.where(valid, 0.0, NEG).astype(f32)
    lo = lax.broadcasted_iota(jnp.int32, (DEC, LANES), 1) < HEAD_DIM
    zero = jnp.zeros((DEC, LANES), f32)
    pad_rows = jnp.zeros((KEYS - WINDOW - DEC, LANES), f32)
    sink = sink_ref[...]

    def body(s, carry):
        r0 = pl.multiple_of(s * DEC, DEC)
        qs = q_scr[pl.ds(r0, DEC), :]
        pieces = [None] * N_HEADS
        for c in range(CHUNKS):
            qc = qs[:, c * LANES:(c + 1) * LANES]
            qr = pltpu.roll(qc, HEAD_DIM, 1)
            if c < CHUNKS_PER_KV:
                pieces[2 * c], pieces[2 * c + 1] = jnp.where(lo, qc, zero), jnp.where(lo, qr, zero)
            else:
                pieces[2 * c], pieces[2 * c + 1] = jnp.where(lo, zero, qr), jnp.where(lo, zero, qc)
        qrows = jnp.concatenate(pieces, axis=0).astype(bf16)
        kf = jnp.concatenate([ck_ref[s], k_scr[pl.ds(r0, DEC), :], pad_rows], axis=0).astype(bf16)
        vf = jnp.concatenate([cv_ref[s], v_scr[pl.ds(r0, DEC), :], pad_rows], axis=0).astype(bf16)
        sc = lax.dot_general(qrows, kf, (((1,), (1,)), ((), ())), preferred_element_type=f32) + bias
        m = jnp.maximum(jnp.broadcast_to(jnp.max(sc, axis=1, keepdims=True), sink.shape), sink)
        e = jnp.exp(sc - jnp.concatenate([m] * (KEYS // LANES), axis=1))
        den = jnp.broadcast_to(jnp.sum(e, axis=1, keepdims=True), sink.shape) + jnp.exp(sink - m)
        o = jnp.dot(e.astype(bf16), vf, preferred_element_type=f32) * (1.0 / den)
        outs = []
        for c in range(CHUNKS):
            ev, od = o[2 * c * DEC:(2 * c + 1) * DEC], o[(2 * c + 1) * DEC:(2 * c + 2) * DEC]
            if c < CHUNKS_PER_KV:
                outs.append(jnp.where(lo, ev, pltpu.roll(od, HEAD_DIM, 1)))
            else:
                outs.append(jnp.where(lo, pltpu.roll(ev, HEAD_DIM, 1), od))
        attn_scr[pl.ds(r0, DEC), :] = jnp.concatenate(outs, axis=1)
        return carry

    lax.fori_loop(0, SEQ_BLOCK, body, 0, unroll=SEQ_UNROLL)

    z_attn = attn_scr[...] * _silu(p_ref[:, GA0:GA0 + D_ATTN])
    o_attn = jnp.dot(z_attn.astype(bf16), wout_ref[0:D_ATTN, :], preferred_element_type=f32)
    y_ref[...] = _rms_norm(x_ref[...] + (o_conv + o_attn), gf_ref[...])


def _sample_mixer(p, x, ck, cv, st, tabs, sink_col, w_dw, b_dw, ln_g, ln_b, w_out, gf):
    nseq = ck.shape[0]
    rows = SEQ_BLOCK * DEC

    def const(shape):
        return pl.BlockSpec(shape, lambda i: (0,) * len(shape))

    return pl.pallas_call(
        _sample_mixer_kernel,
        grid=(nseq // SEQ_BLOCK,),
        in_specs=[
            pl.BlockSpec((rows, D_PROJ), lambda i: (i, 0)),
            pl.BlockSpec((rows, D_MODEL), lambda i: (i, 0)),
            pl.BlockSpec((SEQ_BLOCK, WINDOW, KV_WIDTH), lambda i: (i, 0, 0)),
            pl.BlockSpec((SEQ_BLOCK, WINDOW, KV_WIDTH), lambda i: (i, 0, 0)),
            pl.BlockSpec((SEQ_BLOCK, PREFIX, D_CONV), lambda i: (i, 0, 0)),
            const((rows, LANES)), const((rows, LANES)), const((rows, LANES)),
            const((N_HEADS * DEC, LANES)),
            const((CONV_WIDTH, D_CONV)), const((1, D_CONV)), const((1, D_CONV)), const((1, D_CONV)),
            pl.BlockSpec((D_ATTN + D_CONV, D_MODEL), lambda i: (0, 0), pipeline_mode=pl.Buffered(1)),
            const((1, D_MODEL)),
        ],
        out_specs=[
            pl.BlockSpec((rows, D_MODEL), lambda i: (i, 0)),
            pl.BlockSpec((SEQ_BLOCK, WINDOW, KV_WIDTH), lambda i: (i, 0, 0)),
            pl.BlockSpec((SEQ_BLOCK, WINDOW, KV_WIDTH), lambda i: (i, 0, 0)),
            pl.BlockSpec((SEQ_BLOCK, PREFIX, D_CONV), lambda i: (i, 0, 0)),
        ],
        out_shape=[
            jax.ShapeDtypeStruct((nseq * DEC, D_MODEL), f32),
            jax.ShapeDtypeStruct((nseq, WINDOW, KV_WIDTH), f32),
            jax.ShapeDtypeStruct((nseq, WINDOW, KV_WIDTH), f32),
            jax.ShapeDtypeStruct((nseq, PREFIX, D_CONV), f32),
        ],
        scratch_shapes=[
            pltpu.VMEM((rows, D_ATTN), f32),
            pltpu.VMEM((rows, KV_WIDTH), f32),
            pltpu.VMEM((rows, KV_WIDTH), f32),
            pltpu.VMEM((rows, D_ATTN), f32),
            pltpu.VMEM((SEQ_BLOCK, PAD + DEC, D_CONV), f32),
        ],
        compiler_params=pltpu.CompilerParams(dimension_semantics=("arbitrary",), vmem_limit_bytes=VMEM_LIMIT),
    )(p, x, ck, cv, st, *tabs, sink_col, w_dw, b_dw, ln_g, ln_b, w_out, gf)


def kernel(x_prompt, x_sample, cache_k_win, cache_v_win, state_conv, meta_tokens, norm_gain, w_in,
           attn_sinks, w_dw, b_dw, conv_norm_gain, conv_norm_bias, w_out, final_norm_gain):
    nb, seq, _ = x_prompt.shape
    nseq, dec, _ = x_sample.shape
    assert norm_gain.shape[0] == 1 and dec == DEC and seq % TQ == 0 and nseq % SEQ_BLOCK == 0

    g = norm_gain[0][None]
    w_out_b = w_out[0].astype(bf16)
    sinks = attn_sinks[0]
    wdw, bdw = w_dw[0], b_dw[0][None]
    lng, lnb = conv_norm_gain[0][None], conv_norm_bias[0][None]
    gf = final_norm_gain[None]

    xs = x_sample.reshape(nseq * dec, D_MODEL)
    p_s, p_meta, w_in_b = _project_sample(xs, meta_tokens, g, w_in[0])

    tabs = _rope_tables(N_META + jnp.arange(seq))
    mtabs = _rope_tables(jnp.arange(N_META))
    y_prompt, nk_p, nv_p, nc_p = _prompt_call(sinks, x_prompt, tabs, p_meta, mtabs, g, w_in_b,
                                               wdw, bdw, lng, lnb, w_out_b, gf)

    stabs = _rope_tables(PAST_LEN + jnp.arange(SEQ_BLOCK * dec) % dec)
    sink_col = jnp.broadcast_to(jnp.repeat(sinks, dec)[:, None], (N_HEADS * dec, LANES))
    ck = cache_k_win[0].reshape(nseq, WINDOW, KV_WIDTH)
    cv = cache_v_win[0].reshape(nseq, WINDOW, KV_WIDTH)
    y_s, nk_s, nv_s, nc_s = _sample_mixer(p_s, xs, ck, cv, state_conv[0], stabs, sink_col, wdw, bdw, lng, lnb,
                                          w_out_b, gf)
    y_sample = y_s.reshape(nseq, dec, D_MODEL)

    kv_shape = (1, -1, WINDOW, N_KV_HEADS, HEAD_DIM)
    return (y_prompt, y_sample, nk_p.reshape(kv_shape), nv_p.reshape(kv_shape), nc_p[None],
            nk_s.reshape(kv_shape), nv_s.reshape(kv_shape), nc_s[None])
```
